```python
import jax
import jax.numpy as jnp
from jax import lax
import numpy as np

D_MODEL = 2048
BATCH = 4
SEQ = 4096
DEPTH = 4

RET_HEAD_DIM = 256
RET_HEADS = (D_MODEL // 2) // RET_HEAD_DIM
RET_WIDTH = RET_HEADS * RET_HEAD_DIM
RET_CHUNK = 128
POOL_WINDOWS = (2, 4, 8, 16)
POOL_WIDTH = D_MODEL - RET_WIDTH
POOL_GROUP = POOL_WIDTH // len(POOL_WINDOWS)
EVEN_IN = 4 * RET_WIDTH + POOL_WIDTH
ATTN_HEAD_DIM = 64
ATTN_Q_HEADS = D_MODEL // ATTN_HEAD_DIM
ATTN_KV_HEADS = ATTN_Q_HEADS // 8
ATTN_GROUP = ATTN_Q_HEADS // ATTN_KV_HEADS
ATTN_Q_WIDTH = ATTN_Q_HEADS * ATTN_HEAD_DIM
ATTN_KV_WIDTH = ATTN_KV_HEADS * ATTN_HEAD_DIM
ODD_IN = ATTN_Q_WIDTH + 2 * ATTN_KV_WIDTH
WINDOW = 128
ATTN_BLOCK = 128
D_FF = 5632
CONV_WIDTH = 3
N_EVEN = (DEPTH + 1) // 2
N_ODD = DEPTH // 2
DEEPNORM_ALPHA = (2 * DEPTH) ** 0.25
DEEPNORM_BETA = (8 * DEPTH) ** -0.25
LN_EPS = 1e-5

kernel_name = 'hybrid_retention_pool_swa_sink_convffn'


def layer_norm(x, g, b):
    xf = x.astype(jnp.float32)
    mu = jnp.mean(xf, axis=-1, keepdims=True)
    var = jnp.mean(jnp.square(xf - mu), axis=-1, keepdims=True)
    return ((xf - mu) * lax.rsqrt(var + LN_EPS) * g + b).astype(x.dtype)


def head_norm(y):
    mu = jnp.mean(y, axis=-1, keepdims=True)
    var = jnp.mean(jnp.square(y - mu), axis=-1, keepdims=True)
    return (y - mu) * lax.rsqrt(var + LN_EPS)


def retention_chunkwise(q, k, v):
    B, S, H, dk = q.shape
    C = RET_CHUNK
    N = S // C
    log_gamma = jnp.log1p(-(2.0 ** (-5.0 - jnp.arange(H, dtype=jnp.float32))))
    qc = q.reshape(B, N, C, H, dk)
    kc = (k * dk ** -0.5).reshape(B, N, C, H, dk)
    vc = v.reshape(B, N, C, H, v.shape[-1])
    pos = jnp.arange(C, dtype=jnp.float32)
    diff = pos[:, None] - pos[None, :]
    decay_in = jnp.where(diff[None] >= 0,
                         jnp.exp(log_gamma[:, None, None] * jnp.maximum(diff, 0.0)[None]), 0.0)
    scores = jnp.einsum('bnihd,bnjhd->bnhij', qc, kc) * decay_in
    inner = jnp.einsum('bnhij,bnjhe->bnihe', scores, vc)
    zeta = jnp.exp(log_gamma[:, None] * (C - 1.0 - pos)[None, :])
    kv = jnp.einsum('bnjhd,hj,bnjhe->bnhde', kc, zeta, vc)
    chunk_decay = jnp.exp(log_gamma * C)[:, None, None]

    def step(state, kv_n):
        return state * chunk_decay + kv_n, state

    init = jnp.zeros((B, H, dk, vc.shape[-1]), jnp.float32)
    _, prev = lax.scan(step, init, jnp.moveaxis(kv, 1, 0))
    prev = jnp.moveaxis(prev, 0, 1)
    xi = jnp.exp(log_gamma[:, None] * (pos + 1.0)[None, :])
    cross = jnp.einsum('bnihd,bnhde,hi->bnihe', qc, prev, xi)
    return (inner + cross).reshape(B, S, H, vc.shape[-1])


def multiscale_pool(u, pool_w, pool_scale):
    B, S, _ = u.shape
    uf = u.astype(jnp.float32)
    cs = jnp.cumsum(uf, axis=1)
    t = jnp.arange(S)
    outs = []
    for gi, w in enumerate(POOL_WINDOWS):
        lo, hi = gi * POOL_GROUP, (gi + 1) * POOL_GROUP
        csg = cs[..., lo:hi]
        lagged = jnp.pad(csg, ((0, 0), (w, 0), (0, 0)))[:, :S]
        count = jnp.minimum(t + 1, w).astype(jnp.float32)[None, :, None]
        pooled = (csg - lagged) / count - uf[..., lo:hi]
        outs.append(pooled @ pool_w[gi].astype(jnp.float32))
    return (jnp.concatenate(outs, axis=-1) * pool_scale.astype(jnp.float32)).astype(u.dtype)


def retention_pool_mixer(x, w_in, pool_w, pool_scale, w_out):
    B, S, _ = x.shape
    h = x @ w_in
    q, k, v, g, u = jnp.split(h, [RET_WIDTH, 2 * RET_WIDTH, 3 * RET_WIDTH, 4 * RET_WIDTH], axis=-1)
    heads = lambda a: a.astype(jnp.float32).reshape(B, S, RET_HEADS, RET_HEAD_DIM)
    y = head_norm(retention_chunkwise(heads(q), heads(k), heads(v))).reshape(B, S, RET_WIDTH)
    y_ret = (jax.nn.silu(g.astype(jnp.float32)) * y).astype(x.dtype)
    y_pool = multiscale_pool(u, pool_w, pool_scale)
    return jnp.concatenate([y_ret, y_pool], axis=-1) @ w_out


def alibi_slopes(n):
    start = 2.0 ** (-8.0 / n)
    return start ** jnp.arange(1, n + 1, dtype=jnp.float32)


def with_prev_block(a):
    pad = [(0, 0), (1, 0)] + [(0, 0)] * (a.ndim - 2)
    prev = jnp.pad(a[:, :-1], pad)
    return jnp.concatenate([prev, a], axis=2)


def swa_sink_attention(x, w_qkv, b_qkv, sinks, w_out):
    B, S, _ = x.shape
    L = ATTN_BLOCK
    NB = S // L
    h = x @ w_qkv + b_qkv
    q, k, v = jnp.split(h, [ATTN_Q_WIDTH, ATTN_Q_WIDTH + ATTN_KV_WIDTH], axis=-1)
    q = q.reshape(B, NB, L, ATTN_KV_HEADS, ATTN_GROUP, ATTN_HEAD_DIM)
    kb = with_prev_block(k.reshape(B, NB, L, ATTN_KV_HEADS, ATTN_HEAD_DIM))
    vb = with_prev_block(v.reshape(B, NB, L, ATTN_KV_HEADS, ATTN_HEAD_DIM))
    scores = jnp.einsum('bnikgd,bnjkd->bnkgij', q, kb).astype(jnp.float32) * ATTN_HEAD_DIM ** -0.5
    i = jnp.arange(L)
    j = jnp.arange(2 * L)
    delta = L + i[:, None] - j[None, :]
    s_pos = jnp.arange(NB)[:, None] * L - L + j[None, :]
    valid = (delta >= 0)[None] & (delta < WINDOW)[None] & (s_pos >= 0)[:, None, :]
    slopes = alibi_slopes(ATTN_Q_HEADS).reshape(ATTN_KV_HEADS, ATTN_GROUP)
    bias = -slopes[:, :, None, None] * delta.astype(jnp.float32)
    scores = jnp.where(valid[None, :, None, None], scores + bias[None, None], -jnp.inf)
    sink = sinks.astype(jnp.float32).reshape(ATTN_KV_HEADS, ATTN_GROUP)[None, None, :, :, None, None]
    m = jnp.maximum(jnp.max(scores, axis=-1, keepdims=True), sink)
    p = jnp.exp(scores - m)
    probs = p / (jnp.sum(p, axis=-1, keepdims=True) + jnp.exp(sink - m))
    o = jnp.einsum('bnkgij,bnjkd->bnikgd', probs.astype(vb.dtype), vb)
    return o.reshape(B, S, ATTN_Q_WIDTH) @ w_out


def conv_glu_ffn(x, w_in, conv_w, conv_b, w_out):
    S = x.shape[1]
    h = x @ w_in
    hp = jnp.pad(h, ((0, 0), (CONV_WIDTH - 1, 0), (0, 0)))
    hc = conv_b + hp[:, 0:S] * conv_w[0]
    for kk in range(1, CONV_WIDTH):
        hc = hc + hp[:, kk:kk + S] * conv_w[kk]
    gate, val = jnp.split(hc, 2, axis=-1)
    return (jax.nn.gelu(gate, approximate=False) * val) @ w_out


def setup_inputs(seed: int = 0) -> dict:
    key = jax.random.key(seed)
    ks = jax.random.split(key, 16)

    def nrm(k, shape, scale):
        return jax.random.normal(k, shape, jnp.float32) * scale

    return {
        'x': nrm(ks[0], (BATCH, SEQ, D_MODEL), 1.0),
        'ln_gain': 1.0 + nrm(ks[1], (DEPTH, 2, D_MODEL), 0.02),
        'ln_bias': nrm(ks[2], (DEPTH, 2, D_MODEL), 0.02),
        'even_w_in': nrm(ks[3], (N_EVEN, D_MODEL, EVEN_IN), D_MODEL ** -0.5),
        'pool_w': nrm(ks[4], (N_EVEN, len(POOL_WINDOWS), POOL_GROUP, POOL_GROUP), POOL_GROUP ** -0.5),
        'pool_scale': 1.0 + nrm(ks[5], (N_EVEN, POOL_WIDTH), 0.1),
        'even_w_out': nrm(ks[6], (N_EVEN, D_MODEL, D_MODEL), DEEPNORM_BETA * D_MODEL ** -0.5),
        'attn_w_qkv': nrm(ks[7], (N_ODD, D_MODEL, ODD_IN), D_MODEL ** -0.5),
        'attn_b_qkv': nrm(ks[8], (N_ODD, ODD_IN), 0.02),
        'attn_sinks': nrm(ks[9], (N_ODD, ATTN_Q_HEADS), 0.5),
        'attn_w_out': nrm(ks[10], (N_ODD, ATTN_Q_WIDTH, D_MODEL), DEEPNORM_BETA * ATTN_Q_WIDTH ** -0.5),
        'ffn_w_in': nrm(ks[11], (DEPTH, D_MODEL, 2 * D_FF), D_MODEL ** -0.5),
        'ffn_conv_w': nrm(ks[12], (DEPTH, CONV_WIDTH, 2 * D_FF), CONV_WIDTH ** -0.5),
        'ffn_conv_b': nrm(ks[13], (DEPTH, 2 * D_FF), 0.02),
        'ffn_w_out': nrm(ks[14], (DEPTH, D_FF, D_MODEL), DEEPNORM_BETA * D_FF ** -0.5),
    }


def reference(x, ln_gain, ln_bias, even_w_in, pool_w, pool_scale, even_w_out,
              attn_w_qkv, attn_b_qkv, attn_sinks, attn_w_out,
              ffn_w_in, ffn_conv_w, ffn_conv_b, ffn_w_out):
    for layer in range(DEPTH):
        li = layer // 2
        if layer % 2 == 0:
            mix = retention_pool_mixer(x, even_w_in[li], pool_w[li], pool_scale[li], even_w_out[li])
        else:
            mix = swa_sink_attention(x, attn_w_qkv[li], attn_b_qkv[li], attn_sinks[li], attn_w_out[li])
        x = layer_norm(DEEPNORM_ALPHA * x + mix, ln_gain[layer, 0], ln_bias[layer, 0])
        ffn = conv_glu_ffn(x, ffn_w_in[layer], ffn_conv_w[layer], ffn_conv_b[layer], ffn_w_out[layer])
        x = layer_norm(DEEPNORM_ALPHA * x + ffn, ln_gain[layer, 1], ln_bias[layer, 1])
    return x
```

```python
import functools
import math

import numpy as np
import jax
import jax.numpy as jnp
from jax import lax
from jax.experimental import pallas as pl
from jax.experimental.pallas import tpu as pltpu

F32 = jnp.float32
BF16 = jnp.bfloat16

D_MODEL = 2048
DEPTH = 4
RET_HEAD_DIM = 256
RET_HEADS = 4
RET_WIDTH = RET_HEADS * RET_HEAD_DIM
RET_CHUNK = 128
POOL_WINDOWS = (2, 4, 8, 16)
POOL_GROUP = 256
POOL_WIDTH = len(POOL_WINDOWS) * POOL_GROUP
ATTN_HEAD_DIM = 64
ATTN_Q_HEADS = 32
ATTN_KV_HEADS = 4
ATTN_GROUP = ATTN_Q_HEADS // ATTN_KV_HEADS
ATTN_Q_WIDTH = ATTN_Q_HEADS * ATTN_HEAD_DIM
ATTN_KV_WIDTH = ATTN_KV_HEADS * ATTN_HEAD_DIM
WINDOW = 128
ATTN_BLOCK = 128
D_FF = 5632
CONV_WIDTH = 3
DEEPNORM_ALPHA = (2 * DEPTH) ** 0.25
LN_EPS = 1e-5

V7X_VMEM_BYTES = 64 * 1024 * 1024
VMEM_LIMIT_BYTES = V7X_VMEM_BYTES - 8 * 1024 * 1024
F32_SUBLANES = 8
BF16_SUBLANES = 16
POOL_HALO = 16

_ALIBI_SLOPES = tuple(
    float(v) for v in (np.float32(2.0 ** (-8.0 / ATTN_Q_HEADS))
                       ** np.arange(1, ATTN_Q_HEADS + 1, dtype=np.float32)))


def _params(*semantics):
    return pltpu.CompilerParams(dimension_semantics=semantics,
                                vmem_limit_bytes=VMEM_LIMIT_BYTES)


def _layer_norm(z, gain, bias):
    mu = jnp.mean(z, axis=-1, keepdims=True)
    zc = z - mu
    var = jnp.mean(zc * zc, axis=-1, keepdims=True)
    return zc * lax.rsqrt(var + LN_EPS) * gain + bias


def _matmul_kernel(x_ref, w_ref, b_ref, o_ref, xb_ref):
    @pl.when(pl.program_id(1) == 0)
    def _():
        xb_ref[...] = x_ref[...].astype(BF16)

    acc = jnp.dot(xb_ref[...], w_ref[...], preferred_element_type=F32)
    o_ref[...] = (acc + b_ref[...]).astype(o_ref.dtype)


def _matmul(x, w, b, out_dtype, tm=1024, tn=512):
    m, k = x.shape
    n = w.shape[1]
    return pl.pallas_call(
        _matmul_kernel,
        grid=(m // tm, n // tn),
        in_specs=[
            pl.BlockSpec((tm, k), lambda i, j: (i, 0)),
            pl.BlockSpec((k, tn), lambda i, j: (0, j)),
            pl.BlockSpec((1, tn), lambda i, j: (0, j)),
        ],
        out_specs=pl.BlockSpec((tm, tn), lambda i, j: (i, j)),
        out_shape=jax.ShapeDtypeStruct((m, n), out_dtype),
        scratch_shapes=[pltpu.VMEM((tm, k), BF16)],
        compiler_params=_params("parallel", "arbitrary"),
        name="proj_matmul",
    )(x, w, b)


def _retention_kernel(q_ref, k_ref, v_ref, g_ref, o_ref, state_ref, *, chunks_per_step):
    c_len, dh = RET_CHUNK, RET_HEAD_DIM

    @pl.when(pl.program_id(1) == 0)
    def _():
        state_ref[...] = jnp.zeros_like(state_ref)

    row = lax.broadcasted_iota(jnp.int32, (c_len, c_len), 0)
    col = lax.broadcasted_iota(jnp.int32, (c_len, c_len), 1)
    diff = (row - col).astype(F32)
    pos = lax.broadcasted_iota(jnp.int32, (c_len, dh), 0).astype(F32)
    k_scale = dh ** -0.5
    for h in range(RET_HEADS):
        log_gamma = math.log1p(-(2.0 ** (-5.0 - h)))
        decay = jnp.where(diff >= 0, jnp.exp(log_gamma * jnp.maximum(diff, 0.0)), 0.0) * k_scale
        zeta = jnp.exp(log_gamma * (c_len - 1.0 - pos)) * k_scale
        xi = jnp.exp(log_gamma * (pos + 1.0))
        chunk_decay = math.exp(log_gamma * c_len)
        cols = slice(h * dh, (h + 1) * dh)
        for c in range(chunks_per_step):
            rows = slice(c * c_len, (c + 1) * c_len)
            q = q_ref[rows, cols]
            k = k_ref[rows, cols]
            v = v_ref[rows, cols]
            state = state_ref[h]
            scores = lax.dot_general(q, k, (((1,), (1,)), ((), ())),
                                     preferred_element_type=F32) * decay
            inner = jnp.dot(scores.astype(BF16), v, preferred_element_type=F32)
            cross = jnp.dot(q, state.astype(BF16), preferred_element_type=F32) * xi
            kz = (k.astype(F32) * zeta).astype(BF16)
            kv = lax.dot_general(kz, v, (((0,), (0,)), ((), ())),
                                 preferred_element_type=F32)
            state_ref[h] = state * chunk_decay + kv
            y = inner + cross
            mu = jnp.mean(y, axis=-1, keepdims=True)
            yc = y - mu
            var = jnp.mean(yc * yc, axis=-1, keepdims=True)
            yn = yc * lax.rsqrt(var + LN_EPS)
            g = g_ref[rows, cols]
            o_ref[rows, cols] = (g * jax.nn.sigmoid(g) * yn).astype(o_ref.dtype)


def _retention(qkv, gu, batch, seq, chunks_per_step=4):
    m = qkv.shape[0]
    tr = chunks_per_step * RET_CHUNK
    steps = seq // tr
    row_map = lambda col: (lambda b, n: (b * steps + n, col))
    return pl.pallas_call(
        functools.partial(_retention_kernel, chunks_per_step=chunks_per_step),
        grid=(batch, steps),
        in_specs=[
            pl.BlockSpec((tr, RET_WIDTH), row_map(0)),
            pl.BlockSpec((tr, RET_WIDTH), row_map(1)),
            pl.BlockSpec((tr, RET_WIDTH), row_map(2)),
            pl.BlockSpec((tr, RET_WIDTH), row_map(0)),
        ],
        out_specs=pl.BlockSpec((tr, RET_WIDTH), row_map(0)),
        out_shape=jax.ShapeDtypeStruct((m, RET_WIDTH), BF16),
        scratch_shapes=[pltpu.VMEM((RET_HEADS, RET_HEAD_DIM, RET_HEAD_DIM), F32)],
        compiler_params=_params("parallel", "arbitrary"),
        name="retention",
    )(qkv, qkv, qkv, gu)


def _pool_kernel(u_ref, halo_ref, w_ref, scale_ref, o_ref, *, tiles_per_seq):
    tr = u_ref.shape[0]
    tile_in_seq = pl.program_id(0) % tiles_per_seq
    halo_on = (tile_in_seq > 0).astype(F32)
    t = tile_in_seq * tr + lax.broadcasted_iota(jnp.int32, (tr, POOL_GROUP), 0)
    for gi, win in enumerate(POOL_WINDOWS):
        cols = slice(gi * POOL_GROUP, (gi + 1) * POOL_GROUP)
        u = u_ref[:, cols]
        ext = jnp.concatenate([halo_ref[:, cols] * halo_on, u], axis=0)
        span = 1
        while span < win:
            ext = ext + pltpu.roll(ext, span, axis=0)
            span *= 2
        count = jnp.minimum(t + 1, win).astype(F32)
        pooled = ext[POOL_HALO:, :] / count - u
        y = jnp.dot(pooled.astype(BF16), w_ref[gi], preferred_element_type=F32)
        o_ref[:, cols] = (y * scale_ref[:, cols]).astype(o_ref.dtype)


def _pool(gu, pool_w, pool_scale, seq, tr=512):
    m = gu.shape[0]
    halo_blocks = tr // POOL_HALO
    return pl.pallas_call(
        functools.partial(_pool_kernel, tiles_per_seq=seq // tr),
        grid=(m // tr,),
        in_specs=[
            pl.BlockSpec((tr, POOL_WIDTH), lambda i: (i, 1)),
            pl.BlockSpec((POOL_HALO, POOL_WIDTH),
                         lambda i: (jnp.maximum(i * halo_blocks - 1, 0), 1)),
            pl.BlockSpec((len(POOL_WINDOWS), POOL_GROUP, POOL_GROUP), lambda i: (0, 0, 0)),
            pl.BlockSpec((1, POOL_WIDTH), lambda i: (0, 0)),
        ],
        out_specs=pl.BlockSpec((tr, POOL_WIDTH), lambda i: (i, 0)),
        out_shape=jax.ShapeDtypeStruct((m, POOL_WIDTH), BF16),
        compiler_params=_params("parallel"),
        name="pool",
    )(gu, gu, pool_w, pool_scale)


def _proj_ln_kernel(*refs, n_in):
    y_refs, w_refs = refs[:n_in], refs[n_in:2 * n_in]
    x_ref, gain_ref, bias_ref, o_ref = refs[2 * n_in:]
    acc = jnp.dot(y_refs[0][...], w_refs[0][...], preferred_element_type=F32)
    for y_ref, w_ref in zip(y_refs[1:], w_refs[1:]):
        acc = acc + jnp.dot(y_ref[...], w_ref[...], preferred_element_type=F32)
    z = DEEPNORM_ALPHA * x_ref[...] + acc
    o_ref[...] = _layer_norm(z, gain_ref[...], bias_ref[...])


def _proj_ln(ys, ws, x, gain, bias, tm=512):
    m, d = x.shape
    n_in = len(ys)
    in_specs = [pl.BlockSpec((tm, y.shape[1]), lambda i: (i, 0)) for y in ys]
    in_specs += [pl.BlockSpec(w.shape, lambda i: (0, 0)) for w in ws]
    in_specs += [
        pl.BlockSpec((tm, d), lambda i: (i, 0)),
        pl.BlockSpec((1, d), lambda i: (0, 0)),
        pl.BlockSpec((1, d), lambda i: (0, 0)),
    ]
    return pl.pallas_call(
        functools.partial(_proj_ln_kernel, n_in=n_in),
        grid=(m // tm,),
        in_specs=in_specs,
        out_specs=pl.BlockSpec((tm, d), lambda i: (i, 0)),
        out_shape=jax.ShapeDtypeStruct((m, d), F32),
        compiler_params=_params("parallel"),
        name="proj_ln",
    )(*ys, *ws, x, gain, bias)


def _attn_kernel(sink_ref, q_ref, kv_cur_ref, kv_prev_ref, o_ref, *, blocks_per_seq):
    blk, dh = ATTN_BLOCK, ATTN_HEAD_DIM
    not_first = pl.program_id(0) % blocks_per_seq > 0
    i = lax.broadcasted_iota(jnp.int32, (blk, 2 * blk), 0)
    j = lax.broadcasted_iota(jnp.int32, (blk, 2 * blk), 1)
    delta = blk + i - j
    valid = (delta >= 0) & (delta < WINDOW) & ((j >= blk) | not_first)
    delta_f = delta.astype(F32)
    kv = jnp.concatenate([kv_prev_ref[...], kv_cur_ref[...]], axis=0)
    for kh in range(ATTN_KV_HEADS):
        k = kv[:, kh * dh:(kh + 1) * dh]
        v = kv[:, ATTN_KV_WIDTH + kh * dh:ATTN_KV_WIDTH + (kh + 1) * dh]
        for g in range(ATTN_GROUP):
            h = kh * ATTN_GROUP + g
            cols = slice(h * dh, (h + 1) * dh)
            s = lax.dot_general(q_ref[:, cols], k, (((1,), (1,)), ((), ())),
                                preferred_element_type=F32) * dh ** -0.5
            s = jnp.where(valid, s - _ALIBI_SLOPES[h] * delta_f, -jnp.inf)
            sink = sink_ref[0, h]
            mx = jnp.maximum(jnp.max(s, axis=-1, keepdims=True), sink)
            p = jnp.exp(s - mx)
            denom = jnp.sum(p, axis=-1, keepdims=True) + jnp.exp(sink - mx)
            probs = (p / denom).astype(BF16)
            o_ref[:, cols] = jnp.dot(probs, v, preferred_element_type=F32).astype(o_ref.dtype)


def _attention(q, kv, sinks, seq):
    m = q.shape[0]
    blk = ATTN_BLOCK
    return pl.pallas_call(
        functools.partial(_attn_kernel, blocks_per_seq=seq // blk),
        grid=(m // blk,),
        in_specs=[
            pl.BlockSpec(memory_space=pltpu.SMEM),
            pl.BlockSpec((blk, ATTN_Q_WIDTH), lambda i: (i, 0)),
            pl.BlockSpec((blk, 2 * ATTN_KV_WIDTH), lambda i: (i, 0)),
            pl.BlockSpec((blk, 2 * ATTN_KV_WIDTH), lambda i: (jnp.maximum(i - 1, 0), 0)),
        ],
        out_specs=pl.BlockSpec((blk, ATTN_Q_WIDTH), lambda i: (i, 0)),
        out_shape=jax.ShapeDtypeStruct((m, ATTN_Q_WIDTH), BF16),
        compiler_params=_params("parallel"),
        name="swa_attention",
    )(sinks, q, kv, kv)


def _ffn_kernel(x_ref, halo_ref, wg_ref, wv_ref, cwg_ref, cwv_ref, cbg_ref, cbv_ref,
                wo_ref, gain_ref, bias_ref, o_ref, xb_ref, acc_ref, *, tiles_per_seq):
    j = pl.program_id(1)
    pad = BF16_SUBLANES

    @pl.when(j == 0)
    def _():
        halo_on = (pl.program_id(0) % tiles_per_seq > 0).astype(F32)
        halo = halo_ref[...] * halo_on
        xb_ref[0:pad, :] = jnp.concatenate([jnp.zeros_like(halo), halo], axis=0).astype(BF16)
        xb_ref[pad:, :] = x_ref[...].astype(BF16)

    xb = xb_ref[...]

    def conv(w_ref, cw_ref, cb_ref):
        h = jnp.dot(xb, w_ref[...], preferred_element_type=F32)
        return (cb_ref[...] + h[pad - 2:-2, :] * cw_ref[0:1, :]
                + h[pad - 1:-1, :] * cw_ref[1:2, :] + h[pad:, :] * cw_ref[2:3, :])

    gate = conv(wg_ref, cwg_ref, cbg_ref)
    val = conv(wv_ref, cwv_ref, cbv_ref)
    gelu = 0.5 * gate * (1.0 + lax.erf(gate * math.sqrt(0.5)))
    act = (gelu * val).astype(BF16)
    part = jnp.dot(act, wo_ref[...], preferred_element_type=F32)

    @pl.when(j == 0)
    def _():
        acc_ref[...] = part

    @pl.when(j > 0)
    def _():
        acc_ref[...] += part

    @pl.when(j == pl.num_programs(1) - 1)
    def _():
        z = DEEPNORM_ALPHA * x_ref[...] + acc_ref[...]
        o_ref[...] = _layer_norm(z, gain_ref[...], bias_ref[...])


def _ffn(x, w_in, conv_w, conv_b, w_out, gain, bias, seq, tm=512, tf=512):
    m, d = x.shape
    nf = D_FF // tf
    halo_blocks = tm // F32_SUBLANES
    return pl.pallas_call(
        functools.partial(_ffn_kernel, tiles_per_seq=seq // tm),
        grid=(m // tm, nf),
        in_specs=[
            pl.BlockSpec((tm, d), lambda i, j: (i, 0)),
            pl.BlockSpec((F32_SUBLANES, d),
                         lambda i, j: (jnp.maximum(i * halo_blocks - 1, 0), 0)),
            pl.BlockSpec((d, tf), lambda i, j: (0, j)),
            pl.BlockSpec((d, tf), lambda i, j: (0, nf + j)),
            pl.BlockSpec((CONV_WIDTH, tf), lambda i, j: (0, j)),
            pl.BlockSpec((CONV_WIDTH, tf), lambda i, j: (0, nf + j)),
            pl.BlockSpec((1, tf), lambda i, j: (0, j)),
            pl.BlockSpec((1, tf), lambda i, j: (0, nf + j)),
            pl.BlockSpec((tf, d), lambda i, j: (j, 0)),
            pl.BlockSpec((1, d), lambda i, j: (0, 0)),
            pl.BlockSpec((1, d), lambda i, j: (0, 0)),
        ],
        out_specs=pl.BlockSpec((tm, d), lambda i, j: (i, 0)),
        out_shape=jax.ShapeDtypeStruct((m, d), F32),
        scratch_shapes=[pltpu.VMEM((tm + BF16_SUBLANES, d), BF16),
                        pltpu.VMEM((tm, d), F32)],
        compiler_params=_params("parallel", "arbitrary"),
        name="conv_glu_ffn",
    )(x, x, w_in, w_in, conv_w, conv_w, conv_b, conv_b, w_out, gain, bias)


def kernel(x, ln_gain, ln_bias, even_w_in, pool_w, pool_scale, even_w_out, attn_w_qkv, attn_b_qkv, attn_sinks, attn_w_out, ffn_w_in, ffn_conv_w, ffn_conv_b, ffn_w_out):
    batch, seq, d = x.shape
    xf = x.reshape(batch * seq, d)
    row = lambda a: a.reshape(1, -1)
    for layer in range(DEPTH):
        li = layer // 2
        if layer % 2 == 0:
            w_in = even_w_in[li].astype(BF16)
            qkv_w, gu_w = w_in[:, :3 * RET_WIDTH], w_in[:, 3 * RET_WIDTH:]
            qkv = _matmul(xf, qkv_w, jnp.zeros((1, qkv_w.shape[1]), F32), BF16)
            gu = _matmul(xf, gu_w, jnp.zeros((1, gu_w.shape[1]), F32), F32)
            y_ret = _retention(qkv, gu, batch, seq)
            y_pool = _pool(gu, pool_w[li].astype(BF16), row(pool_scale[li]), seq)
            w_out = even_w_out[li].astype(BF16)
            xf = _proj_ln([y_ret, y_pool], [w_out[:RET_WIDTH], w_out[RET_WIDTH:]], xf,
                          row(ln_gain[layer, 0]), row(ln_bias[layer, 0]))
        else:
            w_qkv = attn_w_qkv[li].astype(BF16)
            b_qkv = row(attn_b_qkv[li])
            q = _matmul(xf, w_qkv[:, :ATTN_Q_WIDTH], b_qkv[:, :ATTN_Q_WIDTH], BF16)
            kv = _matmul(xf, w_qkv[:, ATTN_Q_WIDTH:], b_qkv[:, ATTN_Q_WIDTH:], BF16)
            o = _attention(q, kv, row(attn_sinks[li]), seq)
            xf = _proj_ln([o], [attn_w_out[li].astype(BF16)], xf,
                          row(ln_gain[layer, 0]), row(ln_bias[layer, 0]))
        xf = _ffn(xf, ffn_w_in[layer].astype(BF16), ffn_conv_w[layer], row(ffn_conv_b[layer]),
                  ffn_w_out[layer].astype(BF16), row(ln_gain[layer, 1]), row(ln_bias[layer, 1]), seq)
    return xf.reshape(batch, seq, d)
```

```python
import functools
import math

import numpy as np
import jax
import jax.numpy as jnp
from jax import lax
from jax.experimental import pallas as pl
from jax.experimental.pallas import tpu as pltpu

F32 = jnp.float32
BF16 = jnp.bfloat16

D_MODEL = 2048
DEPTH = 4
RET_HEAD_DIM = 256
RET_HEADS = 4
RET_WIDTH = RET_HEADS * RET_HEAD_DIM
RET_CHUNK = 128
POOL_WINDOWS = (2, 4, 8, 16)
POOL_GROUP = 256
POOL_WIDTH = len(POOL_WINDOWS) * POOL_GROUP
ATTN_HEAD_DIM = 64
ATTN_Q_HEADS = 32
ATTN_KV_HEADS = 4
ATTN_GROUP = ATTN_Q_HEADS // ATTN_KV_HEADS
ATTN_Q_WIDTH = ATTN_Q_HEADS * ATTN_HEAD_DIM
ATTN_KV_WIDTH = ATTN_KV_HEADS * ATTN_HEAD_DIM
WINDOW = 128
ATTN_BLOCK = 128
D_FF = 5632
CONV_WIDTH = 3
DEEPNORM_ALPHA = (2 * DEPTH) ** 0.25
LN_EPS = 1e-5

V7X_VMEM_BYTES = 64 * 1024 * 1024
VMEM_LIMIT_BYTES = V7X_VMEM_BYTES - 8 * 1024 * 1024
F32_SUBLANES = 8
POOL_HALO = 16

_ALIBI_SLOPES = tuple(
    float(v) for v in (np.float32(2.0 ** (-8.0 / ATTN_Q_HEADS))
                       ** np.arange(1, ATTN_Q_HEADS + 1, dtype=np.float32)))


def _params(*semantics):
    return pltpu.CompilerParams(dimension_semantics=semantics,
                                vmem_limit_bytes=VMEM_LIMIT_BYTES)


def _layer_norm(z, gain, bias):
    mu = jnp.mean(z, axis=-1, keepdims=True)
    zc = z - mu
    var = jnp.mean(zc * zc, axis=-1, keepdims=True)
    return zc * lax.rsqrt(var + LN_EPS) * gain + bias


def _matmul_kernel(x_ref, w_ref, b_ref, o_ref, xb_ref):
    @pl.when(pl.program_id(1) == 0)
    def _():
        xb_ref[...] = x_ref[...].astype(BF16)

    acc = jnp.dot(xb_ref[...], w_ref[...], preferred_element_type=F32)
    o_ref[...] = (acc + b_ref[...]).astype(o_ref.dtype)


def _matmul(x, w, b, col_start, n, out_dtype, tm=1024, tn=512):
    m, k = x.shape
    off = col_start // tn
    assert col_start % tn == 0 and n % tn == 0
    return pl.pallas_call(
        _matmul_kernel,
        grid=(m // tm, n // tn),
        in_specs=[
            pl.BlockSpec((tm, k), lambda i, j: (i, 0)),
            pl.BlockSpec((k, tn), lambda i, j: (0, off + j)),
            pl.BlockSpec((1, tn), lambda i, j: (0, off + j)),
        ],
        out_specs=pl.BlockSpec((tm, tn), lambda i, j: (i, j)),
        out_shape=jax.ShapeDtypeStruct((m, n), out_dtype),
        scratch_shapes=[pltpu.VMEM((tm, k), BF16)],
        compiler_params=_params("parallel", "arbitrary"),
        name="proj_matmul",
    )(x, w, b)


def _retention_kernel(q_ref, k_ref, v_ref, g_ref, o_ref, state_ref, *, chunks_per_step):
    c_len, dh = RET_CHUNK, RET_HEAD_DIM

    @pl.when(pl.program_id(1) == 0)
    def _():
        state_ref[...] = jnp.zeros_like(state_ref)

    row = lax.broadcasted_iota(jnp.int32, (c_len, c_len), 0)
    col = lax.broadcasted_iota(jnp.int32, (c_len, c_len), 1)
    diff = (row - col).astype(F32)
    pos = lax.broadcasted_iota(jnp.int32, (c_len, dh), 0).astype(F32)
    k_scale = dh ** -0.5
    for h in range(RET_HEADS):
        log_gamma = math.log1p(-(2.0 ** (-5.0 - h)))
        decay = jnp.where(diff >= 0, jnp.exp(log_gamma * jnp.maximum(diff, 0.0)), 0.0) * k_scale
        zeta = jnp.exp(log_gamma * (c_len - 1.0 - pos)) * k_scale
        xi = jnp.exp(log_gamma * (pos + 1.0))
        chunk_decay = math.exp(log_gamma * c_len)
        cols = slice(h * dh, (h + 1) * dh)
        for c in range(chunks_per_step):
            rows = slice(c * c_len, (c + 1) * c_len)
            q = q_ref[rows, cols]
            k = k_ref[rows, cols]
            v = v_ref[rows, cols]
            state = state_ref[h]
            scores = lax.dot_general(q, k, (((1,), (1,)), ((), ())),
                                     preferred_element_type=F32) * decay
            inner = jnp.dot(scores.astype(BF16), v, preferred_element_type=F32)
            cross = jnp.dot(q, state.astype(BF16), preferred_element_type=F32) * xi
            kz = (k.astype(F32) * zeta).astype(BF16)
            kv = lax.dot_general(kz, v, (((0,), (0,)), ((), ())),
                                 preferred_element_type=F32)
            state_ref[h] = state * chunk_decay + kv
            y = inner + cross
            mu = jnp.mean(y, axis=-1, keepdims=True)
            yc = y - mu
            var = jnp.mean(yc * yc, axis=-1, keepdims=True)
            yn = yc * lax.rsqrt(var + LN_EPS)
            g = g_ref[rows, cols]
            o_ref[rows, cols] = (g * jax.nn.sigmoid(g) * yn).astype(o_ref.dtype)


def _retention(qkv, gu, batch, seq, chunks_per_step=4):
    m = qkv.shape[0]
    tr = chunks_per_step * RET_CHUNK
    steps = seq // tr
    row_map = lambda col: (lambda b, n: (b * steps + n, col))
    return pl.pallas_call(
        functools.partial(_retention_kernel, chunks_per_step=chunks_per_step),
        grid=(batch, steps),
        in_specs=[
            pl.BlockSpec((tr, RET_WIDTH), row_map(0)),
            pl.BlockSpec((tr, RET_WIDTH), row_map(1)),
            pl.BlockSpec((tr, RET_WIDTH), row_map(2)),
            pl.BlockSpec((tr, RET_WIDTH), row_map(0)),
        ],
        out_specs=pl.BlockSpec((tr, RET_WIDTH), row_map(0)),
        out_shape=jax.ShapeDtypeStruct((m, RET_WIDTH), BF16),
        scratch_shapes=[pltpu.VMEM((RET_HEADS, RET_HEAD_DIM, RET_HEAD_DIM), F32)],
        compiler_params=_params("parallel", "arbitrary"),
        name="retention",
    )(qkv, qkv, qkv, gu)


def _pool_kernel(u_ref, halo_ref, w_ref, scale_ref, o_ref, *, tiles_per_seq):
    tr = u_ref.shape[0]
    tile_in_seq = pl.program_id(0) % tiles_per_seq
    halo_on = (tile_in_seq > 0).astype(F32)
    t = tile_in_seq * tr + lax.broadcasted_iota(jnp.int32, (tr, POOL_GROUP), 0)
    for gi, win in enumerate(POOL_WINDOWS):
        cols = slice(gi * POOL_GROUP, (gi + 1) * POOL_GROUP)
        u = u_ref[:, cols]
        ext = jnp.concatenate([halo_ref[:, cols] * halo_on, u], axis=0)
        span = 1
        while span < win:
            ext = ext + pltpu.roll(ext, span, axis=0)
            span *= 2
        count = jnp.minimum(t + 1, win).astype(F32)
        pooled = ext[POOL_HALO:, :] / count - u
        y = jnp.dot(pooled.astype(BF16), w_ref[gi], preferred_element_type=F32)
        o_ref[:, cols] = (y * scale_ref[:, cols]).astype(o_ref.dtype)


def _pool(gu, pool_w, pool_scale, seq, tr=512):
    m = gu.shape[0]
    halo_blocks = tr // POOL_HALO
    return pl.pallas_call(
        functools.partial(_pool_kernel, tiles_per_seq=seq // tr),
        grid=(m // tr,),
        in_specs=[
            pl.BlockSpec((tr, POOL_WIDTH), lambda i: (i, 1)),
            pl.BlockSpec((POOL_HALO, POOL_WIDTH),
                         lambda i: (jnp.maximum(i * halo_blocks - 1, 0), 1)),
            pl.BlockSpec((len(POOL_WINDOWS), POOL_GROUP, POOL_GROUP), lambda i: (0, 0, 0)),
            pl.BlockSpec((1, POOL_WIDTH), lambda i: (0, 0)),
        ],
        out_specs=pl.BlockSpec((tr, POOL_WIDTH), lambda i: (i, 0)),
        out_shape=jax.ShapeDtypeStruct((m, POOL_WIDTH), BF16),
        compiler_params=_params("parallel"),
        name="pool",
    )(gu, gu, pool_w, pool_scale)


def _proj_ln_kernel(*refs, n_in):
    y_refs, w_refs = refs[:n_in], refs[n_in:2 * n_in]
    x_ref, gain_ref, bias_ref, o_ref = refs[2 * n_in:]
    acc = jnp.dot(y_refs[0][...], w_refs[0][...], preferred_element_type=F32)
    for y_ref, w_ref in zip(y_refs[1:], w_refs[1:]):
        acc = acc + jnp.dot(y_ref[...], w_ref[...], preferred_element_type=F32)
    z = DEEPNORM_ALPHA * x_ref[...] + acc
    o_ref[...] = _layer_norm(z, gain_ref[...], bias_ref[...])


def _proj_ln(ys, w, x, gain, bias, tm=512):
    m, d = x.shape
    n_in = len(ys)
    width = ys[0].shape[1]
    assert all(y.shape[1] == width for y in ys) and n_in * width == w.shape[0]
    in_specs = [pl.BlockSpec((tm, width), lambda i: (i, 0)) for _ in ys]
    in_specs += [pl.BlockSpec((width, d), lambda i, s=s: (s, 0)) for s in range(n_in)]
    in_specs += [
        pl.BlockSpec((tm, d), lambda i: (i, 0)),
        pl.BlockSpec((1, d), lambda i: (0, 0)),
        pl.BlockSpec((1, d), lambda i: (0, 0)),
    ]
    return pl.pallas_call(
        functools.partial(_proj_ln_kernel, n_in=n_in),
        grid=(m // tm,),
        in_specs=in_specs,
        out_specs=pl.BlockSpec((tm, d), lambda i: (i, 0)),
        out_shape=jax.ShapeDtypeStruct((m, d), F32),
        compiler_params=_params("parallel"),
        name="proj_ln",
    )(*ys, *([w] * n_in), x, gain, bias)


def _attn_kernel(sink_ref, q_ref, kv_cur_ref, kv_prev_ref, o_ref, *, blocks_per_seq):
    blk, dh = ATTN_BLOCK, ATTN_HEAD_DIM
    lanes = 2 * dh
    pairs = ATTN_GROUP // 2
    not_first = pl.program_id(0) % blocks_per_seq > 0
    i = lax.broadcasted_iota(jnp.int32, (blk, blk), 0)
    j = lax.broadcasted_iota(jnp.int32, (blk, blk), 1)
    upper = j > i
    prev_ok = upper & not_first
    delta_f = jnp.where(upper, blk + i - j, i - j).astype(F32)
    lane = lax.broadcasted_iota(jnp.int32, (2 * blk, lanes), 1)
    low = lane < dh

    def block_diag(ref_prev, ref_cur, tile):
        cols = slice(tile * lanes, (tile + 1) * lanes)
        t = jnp.concatenate([ref_prev[:, cols], ref_cur[:, cols]], axis=0)
        zero = jnp.zeros_like(t)
        return jnp.concatenate([jnp.where(low, t, zero), jnp.where(low, zero, t)], axis=0)

    def scores(kh):
        qs = jnp.concatenate([q_ref[:, (kh * pairs + p) * lanes:(kh * pairs + p + 1) * lanes]
                              for p in range(pairs)], axis=0)
        k_bd = block_diag(kv_prev_ref, kv_cur_ref, kh)
        return lax.dot_general(qs, k_bd, (((1,), (1,)), ((), ())), preferred_element_type=F32)

    def softmax(kh, s_all):
        rows_out = []
        for p in range(pairs):
            parts = []
            for e in range(2):
                h = kh * ATTN_GROUP + 2 * p + e
                blk_s = s_all[p * blk:(p + 1) * blk, e * 2 * blk:(e + 1) * 2 * blk]
                s = jnp.where(upper, jnp.where(prev_ok, blk_s[:, :blk], -jnp.inf), blk_s[:, blk:])
                s = s * dh ** -0.5 - _ALIBI_SLOPES[h] * delta_f
                sink = sink_ref[0, h]
                mx = jnp.maximum(jnp.max(s, axis=-1, keepdims=True), sink)
                pr = jnp.exp(s - mx)
                denom = jnp.sum(pr, axis=-1, keepdims=True) + jnp.exp(sink - mx)
                probs = (pr / denom).astype(BF16)
                zero = jnp.zeros_like(probs)
                parts += [jnp.where(upper, probs, zero), jnp.where(upper, zero, probs)]
            rows_out.append(jnp.concatenate(parts, axis=1))
        return jnp.concatenate(rows_out, axis=0)

    def values(kh, probs):
        v_bd = block_diag(kv_prev_ref, kv_cur_ref, ATTN_KV_HEADS + kh)
        o = jnp.dot(probs, v_bd, preferred_element_type=F32)
        for p in range(pairs):
            cols = slice((kh * pairs + p) * lanes, (kh * pairs + p + 1) * lanes)
            o_ref[:, cols] = o[p * blk:(p + 1) * blk, :].astype(o_ref.dtype)

    s_next = scores(0)
    for kh in range(ATTN_KV_HEADS):
        s_cur = s_next
        if kh + 1 < ATTN_KV_HEADS:
            s_next = scores(kh + 1)
        values(kh, softmax(kh, s_cur))


def _attention(q, kv, sinks, seq):
    m = q.shape[0]
    blk = ATTN_BLOCK
    kvw = kv.shape[1]
    return pl.pallas_call(
        functools.partial(_attn_kernel, blocks_per_seq=seq // blk),
        grid=(m // blk,),
        in_specs=[
            pl.BlockSpec(memory_space=pltpu.SMEM),
            pl.BlockSpec((blk, ATTN_Q_WIDTH), lambda i: (i, 0)),
            pl.BlockSpec((blk, kvw), lambda i: (i, 0)),
            pl.BlockSpec((blk, kvw), lambda i: (jnp.maximum(i - 1, 0), 0)),
        ],
        out_specs=pl.BlockSpec((blk, ATTN_Q_WIDTH), lambda i: (i, 0)),
        out_shape=jax.ShapeDtypeStruct((m, ATTN_Q_WIDTH), BF16),
        compiler_params=_params("parallel"),
        name="swa_attention",
    )(sinks, q, kv, kv)


def _dup_heads(a):
    lead = a.shape[:-1]
    a = a.reshape(*lead, ATTN_KV_HEADS, 1, ATTN_HEAD_DIM)
    a = jnp.broadcast_to(a, (*lead, ATTN_KV_HEADS, 2, ATTN_HEAD_DIM))
    return a.reshape(*lead, 2 * ATTN_KV_WIDTH)


def _ffn_kernel(x_ref, wg_ref, wv_ref, cwg_ref, cwv_ref, cbg_ref, cbv_ref,
                wo_ref, gain_ref, bias_ref, o_ref, xb_ref, h_ref, tail_ref, act_ref, acc_ref,
                *, tiles_per_seq, row_block, sub_rows, sub_cols):
    i = pl.program_id(0)
    j = pl.program_id(1)
    tm, tf = act_ref.shape
    top = F32_SUBLANES

    @pl.when(j == 0)
    def _():
        x = x_ref[...]
        xb_ref[...] = x.astype(BF16)
        acc_ref[...] = DEEPNORM_ALPHA * x

    seq_start = i % tiles_per_seq == 0

    @pl.when(seq_start)
    def _():
        h_ref[0:top, :] = jnp.zeros((top, 2 * tf), F32)

    @pl.when(jnp.logical_not(seq_start))
    def _():
        h_ref[0:top, :] = tail_ref[j]

    def in_proj(r):
        rows = slice(r * row_block, (r + 1) * row_block)
        hrows = slice(top + r * row_block, top + (r + 1) * row_block)
        xb = xb_ref[rows, :]
        h_ref[hrows, 0:tf] = jnp.dot(xb, wg_ref[...], preferred_element_type=F32)
        h_ref[hrows, tf:] = jnp.dot(xb, wv_ref[...], preferred_element_type=F32)

    def conv_act(r):
        for c0 in range(0, tf, sub_cols):
            cg = slice(c0, c0 + sub_cols)
            cv = slice(tf + c0, tf + c0 + sub_cols)
            for r0 in range(r * row_block, (r + 1) * row_block, sub_rows):
                def conv(cols, wcols, cw_ref, cb_ref):
                    return (cb_ref[:, wcols]
                            + h_ref[top + r0 - 2:top + r0 - 2 + sub_rows, cols] * cw_ref[0:1, wcols]
                            + h_ref[top + r0 - 1:top + r0 - 1 + sub_rows, cols] * cw_ref[1:2, wcols]
                            + h_ref[top + r0:top + r0 + sub_rows, cols] * cw_ref[2:3, wcols])
                gate = conv(cg, cg, cwg_ref, cbg_ref)
                val = conv(cv, cg, cwv_ref, cbv_ref)
                gelu = 0.5 * gate * (1.0 + lax.erf(gate * math.sqrt(0.5)))
                act_ref[r0:r0 + sub_rows, cg] = (gelu * val).astype(BF16)

    def out_proj(r):
        rows = slice(r * row_block, (r + 1) * row_block)
        acc_ref[rows, :] += jnp.dot(act_ref[rows, :], wo_ref[...], preferred_element_type=F32)

    n_blocks = tm // row_block
    for step in range(n_blocks + 2):
        if step < n_blocks:
            in_proj(step)
        if 0 <= step - 1 < n_blocks:
            conv_act(step - 1)
        if 0 <= step - 2 < n_blocks:
            out_proj(step - 2)

    tail_ref[j] = h_ref[tm:tm + top, :]

    @pl.when(j == pl.num_programs(1) - 1)
    def _():
        o_ref[...] = _layer_norm(acc_ref[...], gain_ref[...], bias_ref[...])


def _ffn(x, w_in, conv_w, conv_b, w_out, gain, bias, seq, tm=512, tf=512,
         row_block=128, sub_rows=32, sub_cols=256):
    m, d = x.shape
    nf = D_FF // tf
    return pl.pallas_call(
        functools.partial(_ffn_kernel, tiles_per_seq=seq // tm, row_block=row_block,
                          sub_rows=sub_rows, sub_cols=sub_cols),
        grid=(m // tm, nf),
        in_specs=[
            pl.BlockSpec((tm, d), lambda i, j: (i, 0)),
            pl.BlockSpec((d, tf), lambda i, j: (0, j)),
            pl.BlockSpec((d, tf), lambda i, j: (0, nf + j)),
            pl.BlockSpec((CONV_WIDTH, tf), lambda i, j: (0, j)),
            pl.BlockSpec((CONV_WIDTH, tf), lambda i, j: (0, nf + j)),
            pl.BlockSpec((1, tf), lambda i, j: (0, j)),
            pl.BlockSpec((1, tf), lambda i, j: (0, nf + j)),
            pl.BlockSpec((tf, d), lambda i, j: (j, 0)),
            pl.BlockSpec((1, d), lambda i, j: (0, 0)),
            pl.BlockSpec((1, d), lambda i, j: (0, 0)),
        ],
        out_specs=pl.BlockSpec((tm, d), lambda i, j: (i, 0)),
        out_shape=jax.ShapeDtypeStruct((m, d), F32),
        scratch_shapes=[pltpu.VMEM((tm, d), BF16),
                        pltpu.VMEM((F32_SUBLANES + tm, 2 * tf), F32),
                        pltpu.VMEM((nf, F32_SUBLANES, 2 * tf), F32),
                        pltpu.VMEM((tm, tf), BF16),
                        pltpu.VMEM((tm, d), F32)],
        compiler_params=_params("arbitrary", "arbitrary"),
        name="conv_glu_ffn",
    )(x, w_in, w_in, conv_w, conv_w, conv_b, conv_b, w_out, gain, bias)


def kernel(x, ln_gain, ln_bias, even_w_in, pool_w, pool_scale, even_w_out, attn_w_qkv, attn_b_qkv, attn_sinks, attn_w_out, ffn_w_in, ffn_conv_w, ffn_conv_b, ffn_w_out):
    batch, seq, d = x.shape
    xf = x.reshape(batch * seq, d)
    row = lambda a: a.reshape(1, -1)
    for layer in range(DEPTH):
        li = layer // 2
        if layer % 2 == 0:
            w_in = even_w_in[li].astype(BF16)
            no_bias = jnp.zeros((1, w_in.shape[1]), F32)
            qkv = _matmul(xf, w_in, no_bias, 0, 3 * RET_WIDTH, BF16)
            gu = _matmul(xf, w_in, no_bias, 3 * RET_WIDTH, RET_WIDTH + POOL_WIDTH, F32)
            y_ret = _retention(qkv, gu, batch, seq)
            y_pool = _pool(gu, pool_w[li].astype(BF16), row(pool_scale[li]), seq)
            xf = _proj_ln([y_ret, y_pool], even_w_out[li].astype(BF16), xf,
                          row(ln_gain[layer, 0]), row(ln_bias[layer, 0]))
        else:
            w_qkv = attn_w_qkv[li].astype(BF16)
            b_qkv = row(attn_b_qkv[li])
            k_cols = slice(ATTN_Q_WIDTH, ATTN_Q_WIDTH + ATTN_KV_WIDTH)
            v_cols = slice(ATTN_Q_WIDTH + ATTN_KV_WIDTH, ATTN_Q_WIDTH + 2 * ATTN_KV_WIDTH)
            w_kv = jnp.concatenate([_dup_heads(w_qkv[:, k_cols]), _dup_heads(w_qkv[:, v_cols])], axis=1)
            b_kv = jnp.concatenate([_dup_heads(b_qkv[:, k_cols]), _dup_heads(b_qkv[:, v_cols])], axis=1)
            q = _matmul(xf, w_qkv, b_qkv, 0, ATTN_Q_WIDTH, BF16)
            kv = _matmul(xf, w_kv, b_kv, 0, 4 * ATTN_KV_WIDTH, BF16)
            o = _attention(q, kv, row(attn_sinks[li]), seq)
            xf = _proj_ln([o], attn_w_out[li].astype(BF16), xf,
                          row(ln_gain[layer, 0]), row(ln_bias[layer, 0]))
        xf = _ffn(xf, ffn_w_in[layer].astype(BF16), ffn_conv_w[layer], row(ffn_conv_b[layer]),
                  ffn_w_out[layer].astype(BF16), row(ln_gain[layer, 1]), row(ln_bias[layer, 1]), seq)
    return xf.reshape(batch, seq, d)
```

```python
import functools
import math

import numpy as np
import jax
import jax.numpy as jnp
from jax import lax
from jax.experimental import pallas as pl
from jax.experimental.pallas import tpu as pltpu

F32 = jnp.float32
BF16 = jnp.bfloat16

D_MODEL = 2048
DEPTH = 4
RET_HEAD_DIM = 256
RET_HEADS = 4
RET_WIDTH = RET_HEADS * RET_HEAD_DIM
RET_CHUNK = 128
POOL_WINDOWS = (2, 4, 8, 16)
POOL_GROUP = 256
POOL_WIDTH = len(POOL_WINDOWS) * POOL_GROUP
ATTN_HEAD_DIM = 64
ATTN_Q_HEADS = 32
ATTN_KV_HEADS = 4
ATTN_GROUP = ATTN_Q_HEADS // ATTN_KV_HEADS
ATTN_Q_WIDTH = ATTN_Q_HEADS * ATTN_HEAD_DIM
ATTN_KV_WIDTH = ATTN_KV_HEADS * ATTN_HEAD_DIM
WINDOW = 128
ATTN_BLOCK = 128
D_FF = 5632
CONV_WIDTH = 3
DEEPNORM_ALPHA = (2 * DEPTH) ** 0.25
LN_EPS = 1e-5

V7X_VMEM_BYTES = 64 * 1024 * 1024
VMEM_LIMIT_BYTES = V7X_VMEM_BYTES - 8 * 1024 * 1024
LANES = 128
F32_SUBLANES = 8
POOL_HALO = 16

_ALIBI_SLOPES = tuple(
    float(v) for v in (np.float32(2.0 ** (-8.0 / ATTN_Q_HEADS))
                       ** np.arange(1, ATTN_Q_HEADS + 1, dtype=np.float32)))


def _params(*semantics):
    return pltpu.CompilerParams(dimension_semantics=semantics,
                                vmem_limit_bytes=VMEM_LIMIT_BYTES)


def _layer_norm(z, gain, bias):
    mu = jnp.mean(z, axis=-1, keepdims=True)
    zc = z - mu
    var = jnp.mean(zc * zc, axis=-1, keepdims=True)
    return zc * lax.rsqrt(var + LN_EPS) * gain + bias


def _matmul_kernel(x_ref, w_ref, b_ref, o_ref, xb_ref):
    @pl.when(pl.program_id(1) == 0)
    def _():
        xb_ref[...] = x_ref[...].astype(BF16)

    acc = jnp.dot(xb_ref[...], w_ref[...], preferred_element_type=F32)
    o_ref[...] = (acc + b_ref[...]).astype(o_ref.dtype)


def _matmul(x, w, layer, b, col_start, n, out_dtype, tm=1024, tn=512):
    m, k = x.shape
    off = col_start // tn
    assert col_start % tn == 0 and n % tn == 0
    return pl.pallas_call(
        _matmul_kernel,
        grid=(m // tm, n // tn),
        in_specs=[
            pl.BlockSpec((tm, k), lambda i, j: (i, 0)),
            pl.BlockSpec((None, k, tn), lambda i, j: (layer, 0, off + j)),
            pl.BlockSpec((1, tn), lambda i, j: (0, off + j)),
        ],
        out_specs=pl.BlockSpec((tm, tn), lambda i, j: (i, j)),
        out_shape=jax.ShapeDtypeStruct((m, n), out_dtype),
        scratch_shapes=[pltpu.VMEM((tm, k), BF16)],
        compiler_params=_params("parallel", "arbitrary"),
        name="proj_matmul",
    )(x, w, b)


def _retention_kernel(q_ref, k_ref, v_ref, g_ref, o_ref, state_ref, *, chunks_per_step):
    c_len, dh = RET_CHUNK, RET_HEAD_DIM

    @pl.when(pl.program_id(1) == 0)
    def _():
        state_ref[...] = jnp.zeros_like(state_ref)

    row = lax.broadcasted_iota(jnp.int32, (c_len, c_len), 0)
    col = lax.broadcasted_iota(jnp.int32, (c_len, c_len), 1)
    diff = (row - col).astype(F32)
    pos = lax.broadcasted_iota(jnp.int32, (c_len, dh), 0).astype(F32)
    k_scale = dh ** -0.5
    for h in range(RET_HEADS):
        log_gamma = math.log1p(-(2.0 ** (-5.0 - h)))
        decay = jnp.where(diff >= 0, jnp.exp(log_gamma * jnp.maximum(diff, 0.0)), 0.0) * k_scale
        zeta = jnp.exp(log_gamma * (c_len - 1.0 - pos)) * k_scale
        xi = jnp.exp(log_gamma * (pos + 1.0))
        chunk_decay = math.exp(log_gamma * c_len)
        cols = slice(h * dh, (h + 1) * dh)
        for c in range(chunks_per_step):
            rows = slice(c * c_len, (c + 1) * c_len)
            q = q_ref[rows, cols]
            k = k_ref[rows, cols]
            v = v_ref[rows, cols]
            state = state_ref[h]
            scores = lax.dot_general(q, k, (((1,), (1,)), ((), ())),
                                     preferred_element_type=F32) * decay
            inner = jnp.dot(scores.astype(BF16), v, preferred_element_type=F32)
            cross = jnp.dot(q, state.astype(BF16), preferred_element_type=F32) * xi
            kz = (k.astype(F32) * zeta).astype(BF16)
            kv = lax.dot_general(kz, v, (((0,), (0,)), ((), ())),
                                 preferred_element_type=F32)
            state_ref[h] = state * chunk_decay + kv
            y = inner + cross
            mu = jnp.mean(y, axis=-1, keepdims=True)
            yc = y - mu
            var = jnp.mean(yc * yc, axis=-1, keepdims=True)
            yn = yc * lax.rsqrt(var + LN_EPS)
            g = g_ref[rows, cols]
            o_ref[rows, cols] = (g * jax.nn.sigmoid(g) * yn).astype(o_ref.dtype)


def _retention(qkv, gu, batch, seq, chunks_per_step=4):
    m = qkv.shape[0]
    tr = chunks_per_step * RET_CHUNK
    steps = seq // tr
    row_map = lambda col: (lambda b, n: (b * steps + n, col))
    return pl.pallas_call(
        functools.partial(_retention_kernel, chunks_per_step=chunks_per_step),
        grid=(batch, steps),
        in_specs=[
            pl.BlockSpec((tr, RET_WIDTH), row_map(0)),
            pl.BlockSpec((tr, RET_WIDTH), row_map(1)),
            pl.BlockSpec((tr, RET_WIDTH), row_map(2)),
            pl.BlockSpec((tr, RET_WIDTH), row_map(0)),
        ],
        out_specs=pl.BlockSpec((tr, RET_WIDTH), row_map(0)),
        out_shape=jax.ShapeDtypeStruct((m, RET_WIDTH), BF16),
        scratch_shapes=[pltpu.VMEM((RET_HEADS, RET_HEAD_DIM, RET_HEAD_DIM), F32)],
        compiler_params=_params("parallel", "arbitrary"),
        name="retention",
    )(qkv, qkv, qkv, gu)


def _pool_kernel(u_ref, halo_ref, w_ref, scale_ref, o_ref, *, tiles_per_seq):
    tr = u_ref.shape[0]
    tile_in_seq = pl.program_id(0) % tiles_per_seq
    halo_on = (tile_in_seq > 0).astype(F32)
    t = tile_in_seq * tr + lax.broadcasted_iota(jnp.int32, (tr, POOL_GROUP), 0)
    for gi, win in enumerate(POOL_WINDOWS):
        cols = slice(gi * POOL_GROUP, (gi + 1) * POOL_GROUP)
        u = u_ref[:, cols]
        ext = jnp.concatenate([halo_ref[:, cols] * halo_on, u], axis=0)
        span = 1
        while span < win:
            ext = ext + pltpu.roll(ext, span, axis=0)
            span *= 2
        count = jnp.minimum(t + 1, win).astype(F32)
        pooled = ext[POOL_HALO:, :] / count - u
        y = jnp.dot(pooled.astype(BF16), w_ref[gi], preferred_element_type=F32)
        o_ref[:, cols] = (y * scale_ref[:, cols]).astype(o_ref.dtype)


def _pool(gu, pool_w, pool_scale, seq, tr=512):
    m = gu.shape[0]
    halo_blocks = tr // POOL_HALO
    return pl.pallas_call(
        functools.partial(_pool_kernel, tiles_per_seq=seq // tr),
        grid=(m // tr,),
        in_specs=[
            pl.BlockSpec((tr, POOL_WIDTH), lambda i: (i, 1)),
            pl.BlockSpec((POOL_HALO, POOL_WIDTH),
                         lambda i: (jnp.maximum(i * halo_blocks - 1, 0), 1)),
            pl.BlockSpec((len(POOL_WINDOWS), POOL_GROUP, POOL_GROUP), lambda i: (0, 0, 0)),
            pl.BlockSpec((1, POOL_WIDTH), lambda i: (0, 0)),
        ],
        out_specs=pl.BlockSpec((tr, POOL_WIDTH), lambda i: (i, 0)),
        out_shape=jax.ShapeDtypeStruct((m, POOL_WIDTH), BF16),
        compiler_params=_params("parallel"),
        name="pool",
    )(gu, gu, pool_w, pool_scale)


def _proj_ln_kernel(*refs, n_in):
    y_refs, w_refs = refs[:n_in], refs[n_in:2 * n_in]
    x_ref, gain_ref, bias_ref, o_ref = refs[2 * n_in:]
    acc = jnp.dot(y_refs[0][...], w_refs[0][...], preferred_element_type=F32)
    for y_ref, w_ref in zip(y_refs[1:], w_refs[1:]):
        acc = acc + jnp.dot(y_ref[...], w_ref[...], preferred_element_type=F32)
    z = DEEPNORM_ALPHA * x_ref[...] + acc
    o_ref[...] = _layer_norm(z, gain_ref[...], bias_ref[...])


def _proj_ln(ys, w, layer, x, gain, bias, tm=512):
    m, d = x.shape
    n_in = len(ys)
    width = ys[0].shape[1]
    assert all(y.shape[1] == width for y in ys) and n_in * width == w.shape[1]
    in_specs = [pl.BlockSpec((tm, width), lambda i: (i, 0)) for _ in ys]
    in_specs += [pl.BlockSpec((None, width, d), lambda i, s=s: (layer, s, 0)) for s in range(n_in)]
    in_specs += [
        pl.BlockSpec((tm, d), lambda i: (i, 0)),
        pl.BlockSpec((1, d), lambda i: (0, 0)),
        pl.BlockSpec((1, d), lambda i: (0, 0)),
    ]
    return pl.pallas_call(
        functools.partial(_proj_ln_kernel, n_in=n_in),
        grid=(m // tm,),
        in_specs=in_specs,
        out_specs=pl.BlockSpec((tm, d), lambda i: (i, 0)),
        out_shape=jax.ShapeDtypeStruct((m, d), F32),
        compiler_params=_params("parallel"),
        name="proj_ln",
    )(*ys, *([w] * n_in), x, gain, bias)


def _attn_kernel(sink_ref, q_ref, kv_cur_ref, kv_prev_ref, o_ref, *, blocks_per_seq):
    blk, dh = ATTN_BLOCK, ATTN_HEAD_DIM
    lanes = 2 * dh
    pairs = ATTN_GROUP // 2
    not_first = pl.program_id(0) % blocks_per_seq > 0
    i = lax.broadcasted_iota(jnp.int32, (blk, blk), 0)
    j = lax.broadcasted_iota(jnp.int32, (blk, blk), 1)
    upper = j > i
    prev_ok = upper & not_first
    delta_f = jnp.where(upper, blk + i - j, i - j).astype(F32)
    lane = lax.broadcasted_iota(jnp.int32, (2 * blk, lanes), 1)
    low = lane < dh

    def block_diag(ref_prev, ref_cur, tile):
        cols = slice(tile * lanes, (tile + 1) * lanes)
        t = jnp.concatenate([ref_prev[:, cols], ref_cur[:, cols]], axis=0)
        zero = jnp.zeros_like(t)
        return jnp.concatenate([jnp.where(low, t, zero), jnp.where(low, zero, t)], axis=0)

    def scores(kh):
        qs = jnp.concatenate([q_ref[:, (kh * pairs + p) * lanes:(kh * pairs + p + 1) * lanes]
                              for p in range(pairs)], axis=0)
        k_bd = block_diag(kv_prev_ref, kv_cur_ref, kh)
        return lax.dot_general(qs, k_bd, (((1,), (1,)), ((), ())), preferred_element_type=F32)

    def softmax(kh, s_all):
        rows_out = []
        for p in range(pairs):
            parts = []
            for e in range(2):
                h = kh * ATTN_GROUP + 2 * p + e
                blk_s = s_all[p * blk:(p + 1) * blk, e * 2 * blk:(e + 1) * 2 * blk]
                s = jnp.where(upper, jnp.where(prev_ok, blk_s[:, :blk], -jnp.inf), blk_s[:, blk:])
                s = s * dh ** -0.5 - _ALIBI_SLOPES[h] * delta_f
                sink = sink_ref[0, h]
                mx = jnp.maximum(jnp.max(s, axis=-1, keepdims=True), sink)
                pr = jnp.exp(s - mx)
                denom = jnp.sum(pr, axis=-1, keepdims=True) + jnp.exp(sink - mx)
                probs = (pr / denom).astype(BF16)
                zero = jnp.zeros_like(probs)
                parts += [jnp.where(upper, probs, zero), jnp.where(upper, zero, probs)]
            rows_out.append(jnp.concatenate(parts, axis=1))
        return jnp.concatenate(rows_out, axis=0)

    def values(kh, probs):
        v_bd = block_diag(kv_prev_ref, kv_cur_ref, ATTN_KV_HEADS + kh)
        o = jnp.dot(probs, v_bd, preferred_element_type=F32)
        for p in range(pairs):
            cols = slice((kh * pairs + p) * lanes, (kh * pairs + p + 1) * lanes)
            o_ref[:, cols] = o[p * blk:(p + 1) * blk, :].astype(o_ref.dtype)

    s_next = scores(0)
    for kh in range(ATTN_KV_HEADS):
        s_cur = s_next
        if kh + 1 < ATTN_KV_HEADS:
            s_next = scores(kh + 1)
        values(kh, softmax(kh, s_cur))


def _attention(q, kv, sinks, seq):
    m = q.shape[0]
    blk = ATTN_BLOCK
    kvw = kv.shape[1]
    return pl.pallas_call(
        functools.partial(_attn_kernel, blocks_per_seq=seq // blk),
        grid=(m // blk,),
        in_specs=[
            pl.BlockSpec(memory_space=pltpu.SMEM),
            pl.BlockSpec((blk, ATTN_Q_WIDTH), lambda i: (i, 0)),
            pl.BlockSpec((blk, kvw), lambda i: (i, 0)),
            pl.BlockSpec((blk, kvw), lambda i: (jnp.maximum(i - 1, 0), 0)),
        ],
        out_specs=pl.BlockSpec((blk, ATTN_Q_WIDTH), lambda i: (i, 0)),
        out_shape=jax.ShapeDtypeStruct((m, ATTN_Q_WIDTH), BF16),
        compiler_params=_params("parallel"),
        name="swa_attention",
    )(sinks, q, kv, kv)


def _dup_heads(a):
    lead = a.shape[:-1]
    a = a.reshape(*lead, ATTN_KV_HEADS, 1, ATTN_HEAD_DIM)
    a = jnp.broadcast_to(a, (*lead, ATTN_KV_HEADS, 2, ATTN_HEAD_DIM))
    return a.reshape(*lead, 2 * ATTN_KV_WIDTH)


def _ffn_kernel(x_ref, wg_ref, wv_ref, cwg_ref, cwv_ref, cbg_ref, cbv_ref,
                wo_ref, gain_ref, bias_ref, o_ref, xs_ref, xb_ref, tail_ref, acc_ref, *blocks,
                tiles_per_seq, sub_rows, sub_cols):
    n_blocks = len(blocks) // 2
    h_refs, act_refs = blocks[:n_blocks], blocks[n_blocks:]
    i = pl.program_id(0)
    j = pl.program_id(1)
    row_block, tf = act_refs[0].shape
    lane_tiles = xs_ref.shape[0]
    sub = F32_SUBLANES
    tiles = row_block // sub
    top = (CONV_WIDTH - 1) * sub

    def strided_rows(start, stride):
        return jnp.concatenate([xs_ref[c, pl.ds(start, sub, stride=stride), :]
                                for c in range(lane_tiles)], axis=1)

    def to_lane_tile_major(rows, value):
        for c in range(lane_tiles):
            xs_ref[c, rows, :] = value[:, c * LANES:(c + 1) * LANES]

    @pl.when(jnp.logical_and(i == 0, j == 0))
    def _():
        tail_ref[...] = jnp.zeros_like(tail_ref)

    @pl.when(j == 0)
    def _():
        to_lane_tile_major(slice(None), x_ref[...])
        for b in range(n_blocks):
            rows = slice(b * row_block, (b + 1) * row_block)
            x = jnp.concatenate([strided_rows(b * row_block + k, tiles) for k in range(tiles)], axis=0)
            xb_ref[rows, :] = x.astype(BF16)
            acc_ref[rows, :] = DEEPNORM_ALPHA * x

    seq_start = i % tiles_per_seq == 0
    first_sublane = lax.broadcasted_iota(jnp.int32, (sub, tf), 0) == 0

    def in_proj(b, prev_last):
        xb = xb_ref[b * row_block:(b + 1) * row_block, :]
        last = []
        for half, w_ref in enumerate((wg_ref, wv_ref)):
            cols = slice(half * tf, (half + 1) * tf)
            h = jnp.dot(xb, w_ref[...], preferred_element_type=F32)
            h_refs[b][top:, cols] = h
            last.append(h[row_block - top:, :])
            for t in range(CONV_WIDTH - 1):
                rows = slice(t * sub, (t + 1) * sub)
                h_refs[b][rows, cols] = jnp.where(first_sublane,
                                                  pltpu.roll(prev_last[half][rows, :], 1, axis=0),
                                                  pltpu.roll(last[half][rows, :], 1, axis=0))
        return last

    def conv_act(b):
        h_ref = h_refs[b]
        for c0 in range(0, tf, sub_cols):
            cg = slice(c0, c0 + sub_cols)
            cv = slice(tf + c0, tf + c0 + sub_cols)
            for r0 in range(0, row_block, sub_rows):
                def conv(cols, cw_ref, cb_ref):
                    return (cb_ref[:, cg]
                            + h_ref[top + r0 - 2 * sub:top + r0 - 2 * sub + sub_rows, cols] * cw_ref[0:1, cg]
                            + h_ref[top + r0 - sub:top + r0 - sub + sub_rows, cols] * cw_ref[1:2, cg]
                            + h_ref[top + r0:top + r0 + sub_rows, cols] * cw_ref[2:3, cg])
                gate = conv(cg, cwg_ref, cbg_ref)
                val = conv(cv, cwv_ref, cbv_ref)
                gelu = 0.5 * gate * (1.0 + lax.erf(gate * math.sqrt(0.5)))
                act_refs[b][r0:r0 + sub_rows, cg] = (gelu * val).astype(BF16)

    def out_proj(b):
        rows = slice(b * row_block, (b + 1) * row_block)
        acc_ref[rows, :] += jnp.dot(act_refs[b][...], wo_ref[...], preferred_element_type=F32)

    tail = jnp.where(seq_start, 0.0, tail_ref[j])
    last = [tail[:, 0:tf], tail[:, tf:]]
    for step in range(n_blocks + 2):
        if step < n_blocks:
            last = in_proj(step, last)
        if 0 <= step - 1 < n_blocks:
            conv_act(step - 1)
        if 0 <= step - 2 < n_blocks:
            out_proj(step - 2)
    tail_ref[j] = jnp.concatenate(last, axis=1)

    @pl.when(j == pl.num_programs(1) - 1)
    def _():
        for b in range(n_blocks):
            rows = slice(b * row_block, (b + 1) * row_block)
            to_lane_tile_major(rows, _layer_norm(acc_ref[rows, :], gain_ref[...], bias_ref[...]))
        for b in range(n_blocks):
            for m in range(tiles):
                start = b * row_block + sub * ((sub * m) % tiles) + (sub * m) // tiles
                o_ref[b * row_block + sub * m:b * row_block + sub * (m + 1), :] = strided_rows(start, sub)


def _ffn(x, layer, w_in, conv_w, conv_b, w_out, gain, bias, seq, tm=512, tf=512,
         row_block=128, sub_rows=32, sub_cols=256):
    m, d = x.shape
    nf = D_FF // tf
    n_blocks = tm // row_block
    top = (CONV_WIDTH - 1) * F32_SUBLANES
    return pl.pallas_call(
        functools.partial(_ffn_kernel, tiles_per_seq=seq // tm, sub_rows=sub_rows, sub_cols=sub_cols),
        grid=(m // tm, nf),
        in_specs=[
            pl.BlockSpec((tm, d), lambda i, j: (i, 0)),
            pl.BlockSpec((None, d, tf), lambda i, j: (layer, 0, j)),
            pl.BlockSpec((None, d, tf), lambda i, j: (layer, 0, nf + j)),
            pl.BlockSpec((None, CONV_WIDTH, tf), lambda i, j: (layer, 0, j)),
            pl.BlockSpec((None, CONV_WIDTH, tf), lambda i, j: (layer, 0, nf + j)),
            pl.BlockSpec((None, 1, tf), lambda i, j: (layer, 0, j)),
            pl.BlockSpec((None, 1, tf), lambda i, j: (layer, 0, nf + j)),
            pl.BlockSpec((None, tf, d), lambda i, j: (layer, j, 0)),
            pl.BlockSpec((1, d), lambda i, j: (0, 0)),
            pl.BlockSpec((1, d), lambda i, j: (0, 0)),
        ],
        out_specs=pl.BlockSpec((tm, d), lambda i, j: (i, 0)),
        out_shape=jax.ShapeDtypeStruct((m, d), F32),
        scratch_shapes=([pltpu.VMEM((d // LANES, tm, LANES), F32),
                         pltpu.VMEM((tm, d), BF16),
                         pltpu.VMEM((nf, top, 2 * tf), F32),
                         pltpu.VMEM((tm, d), F32)]
                        + [pltpu.VMEM((top + row_block, 2 * tf), F32)] * n_blocks
                        + [pltpu.VMEM((row_block, tf), BF16)] * n_blocks),
        compiler_params=_params("arbitrary", "arbitrary"),
        name="conv_glu_ffn",
    )(x, w_in, w_in, conv_w, conv_w, conv_b, conv_b, w_out, gain, bias)


def kernel(x, ln_gain, ln_bias, even_w_in, pool_w, pool_scale, even_w_out, attn_w_qkv, attn_b_qkv, attn_sinks, attn_w_out, ffn_w_in, ffn_conv_w, ffn_conv_b, ffn_w_out):
    batch, seq, d = x.shape
    xf = x.reshape(batch * seq, d)
    row = lambda a: a.reshape(1, -1)
    even_w_in, even_w_out = even_w_in.astype(BF16), even_w_out.astype(BF16)
    attn_w_qkv, attn_w_out = attn_w_qkv.astype(BF16), attn_w_out.astype(BF16)
    ffn_w_in, ffn_w_out = ffn_w_in.astype(BF16), ffn_w_out.astype(BF16)
    ffn_conv_b = ffn_conv_b[:, None, :]
    no_bias = jnp.zeros((1, even_w_in.shape[2]), F32)
    k_cols = slice(ATTN_Q_WIDTH, ATTN_Q_WIDTH + ATTN_KV_WIDTH)
    v_cols = slice(ATTN_Q_WIDTH + ATTN_KV_WIDTH, ATTN_Q_WIDTH + 2 * ATTN_KV_WIDTH)
    w_kv = jnp.concatenate([_dup_heads(attn_w_qkv[:, :, k_cols]), _dup_heads(attn_w_qkv[:, :, v_cols])], axis=2)
    for layer in range(DEPTH):
        li = layer // 2
        if layer % 2 == 0:
            qkv = _matmul(xf, even_w_in, li, no_bias, 0, 3 * RET_WIDTH, BF16)
            gu = _matmul(xf, even_w_in, li, no_bias, 3 * RET_WIDTH, RET_WIDTH + POOL_WIDTH, F32)
            y_ret = _retention(qkv, gu, batch, seq)
            y_pool = _pool(gu, pool_w[li].astype(BF16), row(pool_scale[li]), seq)
            xf = _proj_ln([y_ret, y_pool], even_w_out, li, xf,
                          row(ln_gain[layer, 0]), row(ln_bias[layer, 0]))
        else:
            b_qkv = row(attn_b_qkv[li])
            b_kv = jnp.concatenate([_dup_heads(b_qkv[:, k_cols]), _dup_heads(b_qkv[:, v_cols])], axis=1)
            q = _matmul(xf, attn_w_qkv, li, b_qkv, 0, ATTN_Q_WIDTH, BF16)
            kv = _matmul(xf, w_kv, li, b_kv, 0, 4 * ATTN_KV_WIDTH, BF16)
            o = _attention(q, kv, row(attn_sinks[li]), seq)
            xf = _proj_ln([o], attn_w_out, li, xf,
                          row(ln_gain[layer, 0]), row(ln_bias[layer, 0]))
        xf = _ffn(xf, layer, ffn_w_in, ffn_conv_w, ffn_conv_b, ffn_w_out,
                  row(ln_gain[layer, 1]), row(ln_bias[layer, 1]), seq)
    return xf.reshape(batch, seq, d)
```

```python
import functools
import math

import numpy as np
import jax
import jax.numpy as jnp
from jax import lax
from jax.experimental import pallas as pl
from jax.experimental.pallas import tpu as pltpu

F32 = jnp.float32
BF16 = jnp.bfloat16

D_MODEL = 2048
DEPTH = 4
RET_HEAD_DIM = 256
RET_HEADS = 4
RET_WIDTH = RET_HEADS * RET_HEAD_DIM
RET_CHUNK = 128
POOL_WINDOWS = (2, 4, 8, 16)
POOL_GROUP = 256
POOL_WIDTH = len(POOL_WINDOWS) * POOL_GROUP
ATTN_HEAD_DIM = 64
ATTN_Q_HEADS = 32
ATTN_KV_HEADS = 4
ATTN_GROUP = ATTN_Q_HEADS // ATTN_KV_HEADS
ATTN_Q_WIDTH = ATTN_Q_HEADS * ATTN_HEAD_DIM
ATTN_KV_WIDTH = ATTN_KV_HEADS * ATTN_HEAD_DIM
WINDOW = 128
ATTN_BLOCK = 128
D_FF = 5632
CONV_WIDTH = 3
DEEPNORM_ALPHA = (2 * DEPTH) ** 0.25
LN_EPS = 1e-5

V7X_VMEM_BYTES = 64 * 1024 * 1024
VMEM_LIMIT_BYTES = V7X_VMEM_BYTES - 8 * 1024 * 1024
LANES = 128
F32_SUBLANES = 8
POOL_HALO = 16

_ALIBI_SLOPES = tuple(
    float(v) for v in (np.float32(2.0 ** (-8.0 / ATTN_Q_HEADS))
                       ** np.arange(1, ATTN_Q_HEADS + 1, dtype=np.float32)))


def _params(*semantics):
    return pltpu.CompilerParams(dimension_semantics=semantics,
                                vmem_limit_bytes=VMEM_LIMIT_BYTES)


def _layer_norm(z, gain, bias):
    mu = jnp.mean(z, axis=-1, keepdims=True)
    zc = z - mu
    var = jnp.mean(zc * zc, axis=-1, keepdims=True)
    return zc * lax.rsqrt(var + LN_EPS) * gain + bias


def _matmul_kernel(x_ref, w_ref, b_ref, o_ref, xb_ref):
    @pl.when(pl.program_id(1) == 0)
    def _():
        xb_ref[...] = x_ref[...].astype(BF16)

    acc = jnp.dot(xb_ref[...], w_ref[...], preferred_element_type=F32)
    o_ref[...] = (acc + b_ref[...]).astype(o_ref.dtype)


def _matmul(x, w, layer, b, col_start, n, out_dtype, tm=1024, tn=1024):
    m, k = x.shape
    off = col_start // tn
    assert col_start % tn == 0 and n % tn == 0
    return pl.pallas_call(
        _matmul_kernel,
        grid=(m // tm, n // tn),
        in_specs=[
            pl.BlockSpec((tm, k), lambda i, j: (i, 0)),
            pl.BlockSpec((None, k, tn), lambda i, j: (layer, 0, off + j)),
            pl.BlockSpec((1, tn), lambda i, j: (0, off + j)),
        ],
        out_specs=pl.BlockSpec((tm, tn), lambda i, j: (i, j)),
        out_shape=jax.ShapeDtypeStruct((m, n), out_dtype),
        scratch_shapes=[pltpu.VMEM((tm, k), BF16)],
        compiler_params=_params("parallel", "arbitrary"),
        name="proj_matmul",
    )(x, w, b)


def _retention_kernel(q_ref, k_ref, v_ref, g_ref, o_ref, state_ref, *, chunks_per_step):
    c_len, dh = RET_CHUNK, RET_HEAD_DIM

    @pl.when(pl.program_id(1) == 0)
    def _():
        state_ref[...] = jnp.zeros_like(state_ref)

    row = lax.broadcasted_iota(jnp.int32, (c_len, c_len), 0)
    col = lax.broadcasted_iota(jnp.int32, (c_len, c_len), 1)
    diff = (row - col).astype(F32)
    pos = lax.broadcasted_iota(jnp.int32, (c_len, dh), 0).astype(F32)
    k_scale = dh ** -0.5
    for h in range(RET_HEADS):
        log_gamma = math.log1p(-(2.0 ** (-5.0 - h)))
        decay = jnp.where(diff >= 0, jnp.exp(log_gamma * jnp.maximum(diff, 0.0)), 0.0) * k_scale
        zeta = jnp.exp(log_gamma * (c_len - 1.0 - pos)) * k_scale
        xi = jnp.exp(log_gamma * (pos + 1.0))
        chunk_decay = math.exp(log_gamma * c_len)
        cols = slice(h * dh, (h + 1) * dh)
        for c in range(chunks_per_step):
            rows = slice(c * c_len, (c + 1) * c_len)
            q = q_ref[rows, cols]
            k = k_ref[rows, cols]
            v = v_ref[rows, cols]
            state = state_ref[h]
            scores = lax.dot_general(q, k, (((1,), (1,)), ((), ())),
                                     preferred_element_type=F32) * decay
            inner = jnp.dot(scores.astype(BF16), v, preferred_element_type=F32)
            cross = jnp.dot(q, state.astype(BF16), preferred_element_type=F32) * xi
            kz = (k.astype(F32) * zeta).astype(BF16)
            kv = lax.dot_general(kz, v, (((0,), (0,)), ((), ())),
                                 preferred_element_type=F32)
            state_ref[h] = state * chunk_decay + kv
            y = inner + cross
            mu = jnp.mean(y, axis=-1, keepdims=True)
            yc = y - mu
            var = jnp.mean(yc * yc, axis=-1, keepdims=True)
            yn = yc * lax.rsqrt(var + LN_EPS)
            g = g_ref[rows, cols]
            o_ref[rows, cols] = (g * jax.nn.sigmoid(g) * yn).astype(o_ref.dtype)


def _retention(qkv, gu, batch, seq, chunks_per_step=4):
    m = qkv.shape[0]
    tr = chunks_per_step * RET_CHUNK
    steps = seq // tr
    row_map = lambda col: (lambda b, n: (b * steps + n, col))
    return pl.pallas_call(
        functools.partial(_retention_kernel, chunks_per_step=chunks_per_step),
        grid=(batch, steps),
        in_specs=[
            pl.BlockSpec((tr, RET_WIDTH), row_map(0)),
            pl.BlockSpec((tr, RET_WIDTH), row_map(1)),
            pl.BlockSpec((tr, RET_WIDTH), row_map(2)),
            pl.BlockSpec((tr, RET_WIDTH), row_map(0)),
        ],
        out_specs=pl.BlockSpec((tr, RET_WIDTH), row_map(0)),
        out_shape=jax.ShapeDtypeStruct((m, RET_WIDTH), BF16),
        scratch_shapes=[pltpu.VMEM((RET_HEADS, RET_HEAD_DIM, RET_HEAD_DIM), F32)],
        compiler_params=_params("parallel", "arbitrary"),
        name="retention",
    )(qkv, qkv, qkv, gu)


def _pool_kernel(u_ref, halo_ref, w_ref, scale_ref, o_ref, *, tiles_per_seq):
    tr = u_ref.shape[0]
    tile_in_seq = pl.program_id(0) % tiles_per_seq
    halo_on = (tile_in_seq > 0).astype(F32)
    t = tile_in_seq * tr + lax.broadcasted_iota(jnp.int32, (tr, POOL_GROUP), 0)
    for gi, win in enumerate(POOL_WINDOWS):
        cols = slice(gi * POOL_GROUP, (gi + 1) * POOL_GROUP)
        u = u_ref[:, cols]
        ext = jnp.concatenate([halo_ref[:, cols] * halo_on, u], axis=0)
        span = 1
        while span < win:
            ext = ext + pltpu.roll(ext, span, axis=0)
            span *= 2
        count = jnp.minimum(t + 1, win).astype(F32)
        pooled = ext[POOL_HALO:, :] / count - u
        y = jnp.dot(pooled.astype(BF16), w_ref[gi], preferred_element_type=F32)
        o_ref[:, cols] = (y * scale_ref[:, cols]).astype(o_ref.dtype)


def _pool(gu, pool_w, pool_scale, seq, tr=512):
    m = gu.shape[0]
    halo_blocks = tr // POOL_HALO
    return pl.pallas_call(
        functools.partial(_pool_kernel, tiles_per_seq=seq // tr),
        grid=(m // tr,),
        in_specs=[
            pl.BlockSpec((tr, POOL_WIDTH), lambda i: (i, 1)),
            pl.BlockSpec((POOL_HALO, POOL_WIDTH),
                         lambda i: (jnp.maximum(i * halo_blocks - 1, 0), 1)),
            pl.BlockSpec((len(POOL_WINDOWS), POOL_GROUP, POOL_GROUP), lambda i: (0, 0, 0)),
            pl.BlockSpec((1, POOL_WIDTH), lambda i: (0, 0)),
        ],
        out_specs=pl.BlockSpec((tr, POOL_WIDTH), lambda i: (i, 0)),
        out_shape=jax.ShapeDtypeStruct((m, POOL_WIDTH), BF16),
        compiler_params=_params("parallel"),
        name="pool",
    )(gu, gu, pool_w, pool_scale)


def _proj_ln_kernel(*refs, n_in):
    y_refs, w_refs = refs[:n_in], refs[n_in:2 * n_in]
    x_ref, gain_ref, bias_ref, o_ref = refs[2 * n_in:]
    acc = jnp.dot(y_refs[0][...], w_refs[0][...], preferred_element_type=F32)
    for y_ref, w_ref in zip(y_refs[1:], w_refs[1:]):
        acc = acc + jnp.dot(y_ref[...], w_ref[...], preferred_element_type=F32)
    z = DEEPNORM_ALPHA * x_ref[...] + acc
    o_ref[...] = _layer_norm(z, gain_ref[...], bias_ref[...])


def _proj_ln(ys, w, layer, x, gain, bias, tm=512):
    m, d = x.shape
    n_in = len(ys)
    width = ys[0].shape[1]
    assert all(y.shape[1] == width for y in ys) and n_in * width == w.shape[1]
    in_specs = [pl.BlockSpec((tm, width), lambda i: (i, 0)) for _ in ys]
    in_specs += [pl.BlockSpec((None, width, d), lambda i, s=s: (layer, s, 0)) for s in range(n_in)]
    in_specs += [
        pl.BlockSpec((tm, d), lambda i: (i, 0)),
        pl.BlockSpec((1, d), lambda i: (0, 0)),
        pl.BlockSpec((1, d), lambda i: (0, 0)),
    ]
    return pl.pallas_call(
        functools.partial(_proj_ln_kernel, n_in=n_in),
        grid=(m // tm,),
        in_specs=in_specs,
        out_specs=pl.BlockSpec((tm, d), lambda i: (i, 0)),
        out_shape=jax.ShapeDtypeStruct((m, d), F32),
        compiler_params=_params("parallel"),
        name="proj_ln",
    )(*ys, *([w] * n_in), x, gain, bias)


def _attn_kernel(sink_ref, q_ref, kv_cur_ref, kv_prev_ref, o_ref, *, blocks_per_seq):
    blk, dh = ATTN_BLOCK, ATTN_HEAD_DIM
    lanes = 2 * dh
    pairs = ATTN_GROUP // 2
    not_first = pl.program_id(0) % blocks_per_seq > 0
    i = lax.broadcasted_iota(jnp.int32, (blk, blk), 0)
    j = lax.broadcasted_iota(jnp.int32, (blk, blk), 1)
    upper = j > i
    prev_ok = upper & not_first
    delta_f = jnp.where(upper, blk + i - j, i - j).astype(F32)
    lane = lax.broadcasted_iota(jnp.int32, (2 * blk, lanes), 1)
    low = lane < dh

    def block_diag(ref_prev, ref_cur, tile):
        cols = slice(tile * lanes, (tile + 1) * lanes)
        t = jnp.concatenate([ref_prev[:, cols], ref_cur[:, cols]], axis=0)
        zero = jnp.zeros_like(t)
        return jnp.concatenate([jnp.where(low, t, zero), jnp.where(low, zero, t)], axis=0)

    def scores(kh):
        qs = jnp.concatenate([q_ref[:, (kh * pairs + p) * lanes:(kh * pairs + p + 1) * lanes]
                              for p in range(pairs)], axis=0)
        k_bd = block_diag(kv_prev_ref, kv_cur_ref, kh)
        return lax.dot_general(qs, k_bd, (((1,), (1,)), ((), ())), preferred_element_type=F32)

    def softmax(kh, s_all):
        rows_out = []
        for p in range(pairs):
            parts = []
            for e in range(2):
                h = kh * ATTN_GROUP + 2 * p + e
                blk_s = s_all[p * blk:(p + 1) * blk, e * 2 * blk:(e + 1) * 2 * blk]
                s = jnp.where(upper, jnp.where(prev_ok, blk_s[:, :blk], -jnp.inf), blk_s[:, blk:])
                s = s * dh ** -0.5 - _ALIBI_SLOPES[h] * delta_f
                sink = sink_ref[0, h]
                mx = jnp.maximum(jnp.max(s, axis=-1, keepdims=True), sink)
                pr = jnp.exp(s - mx)
                denom = jnp.sum(pr, axis=-1, keepdims=True) + jnp.exp(sink - mx)
                probs = (pr / denom).astype(BF16)
                zero = jnp.zeros_like(probs)
                parts += [jnp.where(upper, probs, zero), jnp.where(upper, zero, probs)]
            rows_out.append(jnp.concatenate(parts, axis=1))
        return jnp.concatenate(rows_out, axis=0)

    def values(kh, probs):
        v_bd = block_diag(kv_prev_ref, kv_cur_ref, ATTN_KV_HEADS + kh)
        o = jnp.dot(probs, v_bd, preferred_element_type=F32)
        for p in range(pairs):
            cols = slice((kh * pairs + p) * lanes, (kh * pairs + p + 1) * lanes)
            o_ref[:, cols] = o[p * blk:(p + 1) * blk, :].astype(o_ref.dtype)

    s_next = scores(0)
    for kh in range(ATTN_KV_HEADS):
        s_cur = s_next
        if kh + 1 < ATTN_KV_HEADS:
            s_next = scores(kh + 1)
        values(kh, softmax(kh, s_cur))


def _attention(q, kv, sinks, seq):
    m = q.shape[0]
    blk = ATTN_BLOCK
    kvw = kv.shape[1]
    return pl.pallas_call(
        functools.partial(_attn_kernel, blocks_per_seq=seq // blk),
        grid=(m // blk,),
        in_specs=[
            pl.BlockSpec(memory_space=pltpu.SMEM),
            pl.BlockSpec((blk, ATTN_Q_WIDTH), lambda i: (i, 0)),
            pl.BlockSpec((blk, kvw), lambda i: (i, 0)),
            pl.BlockSpec((blk, kvw), lambda i: (jnp.maximum(i - 1, 0), 0)),
        ],
        out_specs=pl.BlockSpec((blk, ATTN_Q_WIDTH), lambda i: (i, 0)),
        out_shape=jax.ShapeDtypeStruct((m, ATTN_Q_WIDTH), BF16),
        compiler_params=_params("parallel"),
        name="swa_attention",
    )(sinks, q, kv, kv)


def _dup_heads(a):
    lead = a.shape[:-1]
    a = a.reshape(*lead, ATTN_KV_HEADS, 1, ATTN_HEAD_DIM)
    a = jnp.broadcast_to(a, (*lead, ATTN_KV_HEADS, 2, ATTN_HEAD_DIM))
    return a.reshape(*lead, 2 * ATTN_KV_WIDTH)


def _ffn_kernel(x_ref, wg_ref, wv_ref, cwg_ref, cwv_ref, cbg_ref, cbv_ref,
                wo_ref, gain_ref, bias_ref, o_ref, xb_ref, tail_ref, acc_ref, *blocks,
                tiles_per_seq, sub_rows, sub_cols):
    n_blocks = len(blocks) // 3
    h_refs, act_refs, xs_refs = (blocks[k * n_blocks:(k + 1) * n_blocks] for k in range(3))
    i = pl.program_id(0)
    j = pl.program_id(1)
    row_block, tf = act_refs[0].shape
    lane_tiles = xs_refs[0].shape[0]
    sub = F32_SUBLANES
    tiles = row_block // sub
    top = (CONV_WIDTH - 1) * sub

    def strided_rows(b, start, stride):
        return jnp.concatenate([xs_refs[b][c, pl.ds(start, sub, stride=stride), :]
                                for c in range(lane_tiles)], axis=1)

    def to_lane_tile_major(b, value):
        for c in range(lane_tiles):
            xs_refs[b][c, :, :] = value[:, c * LANES:(c + 1) * LANES]

    @pl.when(jnp.logical_and(i == 0, j == 0))
    def _():
        tail_ref[...] = jnp.zeros_like(tail_ref)

    def load_block(b):
        rows = slice(b * row_block, (b + 1) * row_block)
        to_lane_tile_major(b, x_ref[rows, :])
        x = jnp.concatenate([strided_rows(b, k, tiles) for k in range(tiles)], axis=0)
        xb_ref[rows, :] = x.astype(BF16)
        acc_ref[rows, :] = DEEPNORM_ALPHA * x

    def store_block(b):
        rows = slice(b * row_block, (b + 1) * row_block)
        to_lane_tile_major(b, _layer_norm(acc_ref[rows, :], gain_ref[...], bias_ref[...]))
        for m in range(tiles):
            start = sub * ((sub * m) % tiles) + (sub * m) // tiles
            o_ref[b * row_block + sub * m:b * row_block + sub * (m + 1), :] = strided_rows(b, start, sub)

    seq_start = i % tiles_per_seq == 0
    first_sublane = lax.broadcasted_iota(jnp.int32, (sub, tf), 0) == 0

    def in_proj(b, prev_last):
        xb = xb_ref[b * row_block:(b + 1) * row_block, :]
        last = []
        for half, w_ref in enumerate((wg_ref, wv_ref)):
            cols = slice(half * tf, (half + 1) * tf)
            h = jnp.dot(xb, w_ref[...], preferred_element_type=F32)
            h_refs[b][top:, cols] = h
            last.append(h[row_block - top:, :])
            for t in range(CONV_WIDTH - 1):
                rows = slice(t * sub, (t + 1) * sub)
                h_refs[b][rows, cols] = jnp.where(first_sublane,
                                                  pltpu.roll(prev_last[half][rows, :], 1, axis=0),
                                                  pltpu.roll(last[half][rows, :], 1, axis=0))
        return last

    def conv_act(b):
        h_ref = h_refs[b]
        for c0 in range(0, tf, sub_cols):
            cg = slice(c0, c0 + sub_cols)
            cv = slice(tf + c0, tf + c0 + sub_cols)
            for r0 in range(0, row_block, sub_rows):
                def conv(cols, cw_ref, cb_ref):
                    return (cb_ref[:, cg]
                            + h_ref[top + r0 - 2 * sub:top + r0 - 2 * sub + sub_rows, cols] * cw_ref[0:1, cg]
                            + h_ref[top + r0 - sub:top + r0 - sub + sub_rows, cols] * cw_ref[1:2, cg]
                            + h_ref[top + r0:top + r0 + sub_rows, cols] * cw_ref[2:3, cg])
                gate = conv(cg, cwg_ref, cbg_ref)
                val = conv(cv, cwv_ref, cbv_ref)
                gelu = 0.5 * gate * (1.0 + lax.erf(gate * math.sqrt(0.5)))
                act_refs[b][r0:r0 + sub_rows, cg] = (gelu * val).astype(BF16)

    def out_proj(b):
        rows = slice(b * row_block, (b + 1) * row_block)
        acc_ref[rows, :] += jnp.dot(act_refs[b][...], wo_ref[...], preferred_element_type=F32)

    def chunk(first, last_chunk):
        tail = jnp.where(seq_start, 0.0, tail_ref[j])
        last = [tail[:, 0:tf], tail[:, tf:]]
        for step in range(n_blocks + 2):
            if step < n_blocks:
                if first:
                    load_block(step)
                last = in_proj(step, last)
            if 0 <= step - 1 < n_blocks:
                conv_act(step - 1)
            if 0 <= step - 2 < n_blocks:
                out_proj(step - 2)
                if last_chunk:
                    store_block(step - 2)
        tail_ref[j] = jnp.concatenate(last, axis=1)

    nf = pl.num_programs(1)
    pl.when(j == 0)(lambda: chunk(True, False))
    pl.when(jnp.logical_and(j > 0, j < nf - 1))(lambda: chunk(False, False))
    pl.when(j == nf - 1)(lambda: chunk(False, True))


def _ffn(x, layer, w_in, conv_w, conv_b, w_out, gain, bias, seq, tm=512, tf=512,
         row_block=128, sub_rows=32, sub_cols=256):
    m, d = x.shape
    nf = D_FF // tf
    n_blocks = tm // row_block
    top = (CONV_WIDTH - 1) * F32_SUBLANES
    return pl.pallas_call(
        functools.partial(_ffn_kernel, tiles_per_seq=seq // tm, sub_rows=sub_rows, sub_cols=sub_cols),
        grid=(m // tm, nf),
        in_specs=[
            pl.BlockSpec((tm, d), lambda i, j: (i, 0)),
            pl.BlockSpec((None, d, tf), lambda i, j: (layer, 0, j)),
            pl.BlockSpec((None, d, tf), lambda i, j: (layer, 0, nf + j)),
            pl.BlockSpec((None, CONV_WIDTH, tf), lambda i, j: (layer, 0, j)),
            pl.BlockSpec((None, CONV_WIDTH, tf), lambda i, j: (layer, 0, nf + j)),
            pl.BlockSpec((None, 1, tf), lambda i, j: (layer, 0, j)),
            pl.BlockSpec((None, 1, tf), lambda i, j: (layer, 0, nf + j)),
            pl.BlockSpec((None, tf, d), lambda i, j: (layer, j, 0)),
            pl.BlockSpec((1, d), lambda i, j: (0, 0)),
            pl.BlockSpec((1, d), lambda i, j: (0, 0)),
        ],
        out_specs=pl.BlockSpec((tm, d), lambda i, j: (i, 0)),
        out_shape=jax.ShapeDtypeStruct((m, d), F32),
        scratch_shapes=([pltpu.VMEM((tm, d), BF16),
                         pltpu.VMEM((nf, top, 2 * tf), F32),
                         pltpu.VMEM((tm, d), F32)]
                        + [pltpu.VMEM((top + row_block, 2 * tf), F32)] * n_blocks
                        + [pltpu.VMEM((row_block, tf), BF16)] * n_blocks
                        + [pltpu.VMEM((d // LANES, row_block, LANES), F32)] * n_blocks),
        compiler_params=_params("arbitrary", "arbitrary"),
        name="conv_glu_ffn",
    )(x, w_in, w_in, conv_w, conv_w, conv_b, conv_b, w_out, gain, bias)


def kernel(x, ln_gain, ln_bias, even_w_in, pool_w, pool_scale, even_w_out, attn_w_qkv, attn_b_qkv, attn_sinks, attn_w_out, ffn_w_in, ffn_conv_w, ffn_conv_b, ffn_w_out):
    batch, seq, d = x.shape
    xf = x.reshape(batch * seq, d)
    row = lambda a: a.reshape(1, -1)
    even_w_in, even_w_out = even_w_in.astype(BF16), even_w_out.astype(BF16)
    attn_w_qkv, attn_w_out = attn_w_qkv.astype(BF16), attn_w_out.astype(BF16)
    ffn_w_in, ffn_w_out = ffn_w_in.astype(BF16), ffn_w_out.astype(BF16)
    ffn_conv_b = ffn_conv_b[:, None, :]
    no_bias = jnp.zeros((1, even_w_in.shape[2]), F32)
    k_cols = slice(ATTN_Q_WIDTH, ATTN_Q_WIDTH + ATTN_KV_WIDTH)
    v_cols = slice(ATTN_Q_WIDTH + ATTN_KV_WIDTH, ATTN_Q_WIDTH + 2 * ATTN_KV_WIDTH)
    w_kv = jnp.concatenate([_dup_heads(attn_w_qkv[:, :, k_cols]), _dup_heads(attn_w_qkv[:, :, v_cols])], axis=2)
    for layer in range(DEPTH):
        li = layer // 2
        if layer % 2 == 0:
            qkv = _matmul(xf, even_w_in, li, no_bias, 0, 3 * RET_WIDTH, BF16)
            gu = _matmul(xf, even_w_in, li, no_bias, 3 * RET_WIDTH, RET_WIDTH + POOL_WIDTH, F32)
            y_ret = _retention(qkv, gu, batch, seq)
            y_pool = _pool(gu, pool_w[li].astype(BF16), row(pool_scale[li]), seq)
            xf = _proj_ln([y_ret, y_pool], even_w_out, li, xf,
                          row(ln_gain[layer, 0]), row(ln_bias[layer, 0]))
        else:
            b_qkv = row(attn_b_qkv[li])
            b_kv = jnp.concatenate([_dup_heads(b_qkv[:, k_cols]), _dup_heads(b_qkv[:, v_cols])], axis=1)
            q = _matmul(xf, attn_w_qkv, li, b_qkv, 0, ATTN_Q_WIDTH, BF16)
            kv = _matmul(xf, w_kv, li, b_kv, 0, 4 * ATTN_KV_WIDTH, BF16)
            o = _attention(q, kv, row(attn_sinks[li]), seq)
            xf = _proj_ln([o], attn_w_out, li, xf,
                          row(ln_gain[layer, 0]), row(ln_bias[layer, 0]))
        xf = _ffn(xf, layer, ffn_w_in, ffn_conv_w, ffn_conv_b, ffn_w_out,
                  row(ln_gain[layer, 1]), row(ln_bias[layer, 1]), seq)
    return xf.reshape(batch, seq, d)
```

```python
import functools
import math

import numpy as np
import jax
import jax.numpy as jnp
from jax import lax
from jax.experimental import pallas as pl
from jax.experimental.pallas import tpu as pltpu

F32 = jnp.float32
BF16 = jnp.bfloat16

D_MODEL = 2048
DEPTH = 4
RET_HEAD_DIM = 256
RET_HEADS = 4
RET_WIDTH = RET_HEADS * RET_HEAD_DIM
RET_CHUNK = 128
POOL_WINDOWS = (2, 4, 8, 16)
POOL_GROUP = 256
POOL_WIDTH = len(POOL_WINDOWS) * POOL_GROUP
ATTN_HEAD_DIM = 64
ATTN_Q_HEADS = 32
ATTN_KV_HEADS = 4
ATTN_GROUP = ATTN_Q_HEADS // ATTN_KV_HEADS
ATTN_Q_WIDTH = ATTN_Q_HEADS * ATTN_HEAD_DIM
ATTN_KV_WIDTH = ATTN_KV_HEADS * ATTN_HEAD_DIM
WINDOW = 128
ATTN_BLOCK = 128
D_FF = 5632
CONV_WIDTH = 3
DEEPNORM_ALPHA = (2 * DEPTH) ** 0.25
LN_EPS = 1e-5

V7X_VMEM_BYTES = 64 * 1024 * 1024
VMEM_LIMIT_BYTES = V7X_VMEM_BYTES - 8 * 1024 * 1024
FFN_VMEM_LIMIT_BYTES = V7X_VMEM_BYTES - 3 * 1024 * 1024
FFN_STAGING_BUFFERS = 2
LANES = 128
F32_SUBLANES = 8
POOL_HALO = 16

_ALIBI_SLOPES = tuple(
    float(v) for v in (np.float32(2.0 ** (-8.0 / ATTN_Q_HEADS))
                       ** np.arange(1, ATTN_Q_HEADS + 1, dtype=np.float32)))


def _params(*semantics):
    return pltpu.CompilerParams(dimension_semantics=semantics,
                                vmem_limit_bytes=VMEM_LIMIT_BYTES)


def _layer_norm(z, gain, bias):
    mu = jnp.mean(z, axis=-1, keepdims=True)
    zc = z - mu
    var = jnp.mean(zc * zc, axis=-1, keepdims=True)
    return zc * lax.rsqrt(var + LN_EPS) * gain + bias


def _matmul_kernel(x_ref, w_ref, b_ref, o_ref, xb_ref):
    @pl.when(pl.program_id(1) == 0)
    def _():
        xb_ref[...] = x_ref[...].astype(BF16)

    acc = jnp.dot(xb_ref[...], w_ref[...], preferred_element_type=F32)
    o_ref[...] = (acc + b_ref[...]).astype(o_ref.dtype)


def _matmul(x, w, layer, b, col_start, n, out_dtype, tm=1024, tn=1024):
    m, k = x.shape
    off = col_start // tn
    assert col_start % tn == 0 and n % tn == 0
    return pl.pallas_call(
        _matmul_kernel,
        grid=(m // tm, n // tn),
        in_specs=[
            pl.BlockSpec((tm, k), lambda i, j: (i, 0)),
            pl.BlockSpec((None, k, tn), lambda i, j: (layer, 0, off + j)),
            pl.BlockSpec((1, tn), lambda i, j: (0, off + j)),
        ],
        out_specs=pl.BlockSpec((tm, tn), lambda i, j: (i, j)),
        out_shape=jax.ShapeDtypeStruct((m, n), out_dtype),
        scratch_shapes=[pltpu.VMEM((tm, k), BF16)],
        compiler_params=_params("parallel", "arbitrary"),
        name="proj_matmul",
    )(x, w, b)


def _retention_kernel(q_ref, k_ref, v_ref, g_ref, o_ref, state_ref, *, chunks_per_step):
    c_len, dh = RET_CHUNK, RET_HEAD_DIM

    @pl.when(pl.program_id(1) == 0)
    def _():
        state_ref[...] = jnp.zeros_like(state_ref)

    row = lax.broadcasted_iota(jnp.int32, (c_len, c_len), 0)
    col = lax.broadcasted_iota(jnp.int32, (c_len, c_len), 1)
    diff = (row - col).astype(F32)
    pos = lax.broadcasted_iota(jnp.int32, (c_len, dh), 0).astype(F32)
    k_scale = dh ** -0.5
    for h in range(RET_HEADS):
        log_gamma = math.log1p(-(2.0 ** (-5.0 - h)))
        decay = jnp.where(diff >= 0, jnp.exp(log_gamma * jnp.maximum(diff, 0.0)), 0.0) * k_scale
        zeta = jnp.exp(log_gamma * (c_len - 1.0 - pos)) * k_scale
        xi = jnp.exp(log_gamma * (pos + 1.0))
        chunk_decay = math.exp(log_gamma * c_len)
        cols = slice(h * dh, (h + 1) * dh)
        for c in range(chunks_per_step):
            rows = slice(c * c_len, (c + 1) * c_len)
            q = q_ref[rows, cols]
            k = k_ref[rows, cols]
            v = v_ref[rows, cols]
            state = state_ref[h]
            scores = lax.dot_general(q, k, (((1,), (1,)), ((), ())),
                                     preferred_element_type=F32) * decay
            inner = jnp.dot(scores.astype(BF16), v, preferred_element_type=F32)
            cross = jnp.dot(q, state.astype(BF16), preferred_element_type=F32) * xi
            kz = (k.astype(F32) * zeta).astype(BF16)
            kv = lax.dot_general(kz, v, (((0,), (0,)), ((), ())),
                                 preferred_element_type=F32)
            state_ref[h] = state * chunk_decay + kv
            y = inner + cross
            mu = jnp.mean(y, axis=-1, keepdims=True)
            yc = y - mu
            var = jnp.mean(yc * yc, axis=-1, keepdims=True)
            yn = yc * lax.rsqrt(var + LN_EPS)
            g = g_ref[rows, cols]
            o_ref[rows, cols] = (g * jax.nn.sigmoid(g) * yn).astype(o_ref.dtype)


def _retention(qkv, gu, batch, seq, chunks_per_step=4):
    m = qkv.shape[0]
    tr = chunks_per_step * RET_CHUNK
    steps = seq // tr
    row_map = lambda col: (lambda b, n: (b * steps + n, col))
    return pl.pallas_call(
        functools.partial(_retention_kernel, chunks_per_step=chunks_per_step),
        grid=(batch, steps),
        in_specs=[
            pl.BlockSpec((tr, RET_WIDTH), row_map(0)),
            pl.BlockSpec((tr, RET_WIDTH), row_map(1)),
            pl.BlockSpec((tr, RET_WIDTH), row_map(2)),
            pl.BlockSpec((tr, RET_WIDTH), row_map(0)),
        ],
        out_specs=pl.BlockSpec((tr, RET_WIDTH), row_map(0)),
        out_shape=jax.ShapeDtypeStruct((m, RET_WIDTH), BF16),
        scratch_shapes=[pltpu.VMEM((RET_HEADS, RET_HEAD_DIM, RET_HEAD_DIM), F32)],
        compiler_params=_params("parallel", "arbitrary"),
        name="retention",
    )(qkv, qkv, qkv, gu)


def _pool_kernel(u_ref, halo_ref, w_ref, scale_ref, o_ref, *, tiles_per_seq):
    tr = u_ref.shape[0]
    tile_in_seq = pl.program_id(0) % tiles_per_seq
    halo_on = (tile_in_seq > 0).astype(F32)
    t = tile_in_seq * tr + lax.broadcasted_iota(jnp.int32, (tr, POOL_GROUP), 0)
    for gi, win in enumerate(POOL_WINDOWS):
        cols = slice(gi * POOL_GROUP, (gi + 1) * POOL_GROUP)
        u = u_ref[:, cols]
        ext = jnp.concatenate([halo_ref[:, cols] * halo_on, u], axis=0)
        span = 1
        while span < win:
            ext = ext + pltpu.roll(ext, span, axis=0)
            span *= 2
        count = jnp.minimum(t + 1, win).astype(F32)
        pooled = ext[POOL_HALO:, :] / count - u
        y = jnp.dot(pooled.astype(BF16), w_ref[gi], preferred_element_type=F32)
        o_ref[:, cols] = (y * scale_ref[:, cols]).astype(o_ref.dtype)


def _pool(gu, pool_w, pool_scale, seq, tr=512):
    m = gu.shape[0]
    halo_blocks = tr // POOL_HALO
    return pl.pallas_call(
        functools.partial(_pool_kernel, tiles_per_seq=seq // tr),
        grid=(m // tr,),
        in_specs=[
            pl.BlockSpec((tr, POOL_WIDTH), lambda i: (i, 1)),
            pl.BlockSpec((POOL_HALO, POOL_WIDTH),
                         lambda i: (jnp.maximum(i * halo_blocks - 1, 0), 1)),
            pl.BlockSpec((len(POOL_WINDOWS), POOL_GROUP, POOL_GROUP), lambda i: (0, 0, 0)),
            pl.BlockSpec((1, POOL_WIDTH), lambda i: (0, 0)),
        ],
        out_specs=pl.BlockSpec((tr, POOL_WIDTH), lambda i: (i, 0)),
        out_shape=jax.ShapeDtypeStruct((m, POOL_WIDTH), BF16),
        compiler_params=_params("parallel"),
        name="pool",
    )(gu, gu, pool_w, pool_scale)


def _proj_ln_kernel(*refs, n_in):
    y_refs, w_refs = refs[:n_in], refs[n_in:2 * n_in]
    x_ref, gain_ref, bias_ref, o_ref = refs[2 * n_in:]
    acc = jnp.dot(y_refs[0][...], w_refs[0][...], preferred_element_type=F32)
    for y_ref, w_ref in zip(y_refs[1:], w_refs[1:]):
        acc = acc + jnp.dot(y_ref[...], w_ref[...], preferred_element_type=F32)
    z = DEEPNORM_ALPHA * x_ref[...] + acc
    o_ref[...] = _layer_norm(z, gain_ref[...], bias_ref[...])


def _proj_ln(ys, w, layer, x, gain, bias, tm=512):
    m, d = x.shape
    n_in = len(ys)
    width = ys[0].shape[1]
    assert all(y.shape[1] == width for y in ys) and n_in * width == w.shape[1]
    in_specs = [pl.BlockSpec((tm, width), lambda i: (i, 0)) for _ in ys]
    in_specs += [pl.BlockSpec((None, width, d), lambda i, s=s: (layer, s, 0)) for s in range(n_in)]
    in_specs += [
        pl.BlockSpec((tm, d), lambda i: (i, 0)),
        pl.BlockSpec((1, d), lambda i: (0, 0)),
        pl.BlockSpec((1, d), lambda i: (0, 0)),
    ]
    return pl.pallas_call(
        functools.partial(_proj_ln_kernel, n_in=n_in),
        grid=(m // tm,),
        in_specs=in_specs,
        out_specs=pl.BlockSpec((tm, d), lambda i: (i, 0)),
        out_shape=jax.ShapeDtypeStruct((m, d), F32),
        compiler_params=_params("parallel"),
        name="proj_ln",
    )(*ys, *([w] * n_in), x, gain, bias)


def _attn_kernel(sink_ref, q_ref, kv_cur_ref, kv_prev_ref, o_ref, *, steps_per_seq):
    blk, dh = ATTN_BLOCK, ATTN_HEAD_DIM
    lanes = 2 * dh
    pairs = ATTN_GROUP // 2
    i = lax.broadcasted_iota(jnp.int32, (blk, blk), 0)
    j = lax.broadcasted_iota(jnp.int32, (blk, blk), 1)
    upper = j > i
    delta_f = jnp.where(upper, blk + i - j, i - j).astype(F32)
    lane = lax.broadcasted_iota(jnp.int32, (2 * blk, lanes), 1)
    low = lane < dh
    for n in range(q_ref.shape[0] // blk):
        rows = slice(n * blk, (n + 1) * blk)
        if n == 0:
            prev_ok = upper & (pl.program_id(0) % steps_per_seq > 0)
            prev_ref, prev_rows = kv_prev_ref, slice(0, blk)
        else:
            prev_ok = upper
            prev_ref, prev_rows = kv_cur_ref, slice((n - 1) * blk, n * blk)
        _attn_block(sink_ref, q_ref, o_ref, rows, prev_ref, prev_rows, kv_cur_ref,
                    upper, prev_ok, delta_f, low)


def _attn_block(sink_ref, q_ref, o_ref, rows, prev_ref, prev_rows, cur_ref, upper, prev_ok, delta_f, low):
    blk, dh = ATTN_BLOCK, ATTN_HEAD_DIM
    lanes = 2 * dh
    pairs = ATTN_GROUP // 2

    def block_diag(tile):
        cols = slice(tile * lanes, (tile + 1) * lanes)
        t = jnp.concatenate([prev_ref[prev_rows, cols], cur_ref[rows, cols]], axis=0)
        zero = jnp.zeros_like(t)
        return jnp.concatenate([jnp.where(low, t, zero), jnp.where(low, zero, t)], axis=0)

    def scores(kh):
        qs = jnp.concatenate([q_ref[rows, (kh * pairs + p) * lanes:(kh * pairs + p + 1) * lanes]
                              for p in range(pairs)], axis=0)
        k_bd = block_diag(kh)
        return lax.dot_general(qs, k_bd, (((1,), (1,)), ((), ())), preferred_element_type=F32)

    def softmax(kh, s_all):
        rows_out = []
        for p in range(pairs):
            parts = []
            for e in range(2):
                h = kh * ATTN_GROUP + 2 * p + e
                blk_s = s_all[p * blk:(p + 1) * blk, e * 2 * blk:(e + 1) * 2 * blk]
                s = jnp.where(upper, jnp.where(prev_ok, blk_s[:, :blk], -jnp.inf), blk_s[:, blk:])
                s = s * dh ** -0.5 - _ALIBI_SLOPES[h] * delta_f
                sink = sink_ref[0, h]
                mx = jnp.maximum(jnp.max(s, axis=-1, keepdims=True), sink)
                pr = jnp.exp(s - mx)
                denom = jnp.sum(pr, axis=-1, keepdims=True) + jnp.exp(sink - mx)
                probs = (pr / denom).astype(BF16)
                zero = jnp.zeros_like(probs)
                parts += [jnp.where(upper, probs, zero), jnp.where(upper, zero, probs)]
            rows_out.append(jnp.concatenate(parts, axis=1))
        return jnp.concatenate(rows_out, axis=0)

    def values(kh, probs):
        v_bd = block_diag(ATTN_KV_HEADS + kh)
        o = jnp.dot(probs, v_bd, preferred_element_type=F32)
        for p in range(pairs):
            cols = slice((kh * pairs + p) * lanes, (kh * pairs + p + 1) * lanes)
            o_ref[rows, cols] = o[p * blk:(p + 1) * blk, :].astype(o_ref.dtype)

    s_next = scores(0)
    for kh in range(ATTN_KV_HEADS):
        s_cur = s_next
        if kh + 1 < ATTN_KV_HEADS:
            s_next = scores(kh + 1)
        values(kh, softmax(kh, s_cur))


def _attention(q, kv, sinks, seq, blocks_per_step=2):
    m = q.shape[0]
    blk = ATTN_BLOCK
    rows = blocks_per_step * blk
    kvw = kv.shape[1]
    return pl.pallas_call(
        functools.partial(_attn_kernel, steps_per_seq=seq // rows),
        grid=(m // rows,),
        in_specs=[
            pl.BlockSpec(memory_space=pltpu.SMEM),
            pl.BlockSpec((rows, ATTN_Q_WIDTH), lambda i: (i, 0)),
            pl.BlockSpec((rows, kvw), lambda i: (i, 0)),
            pl.BlockSpec((blk, kvw), lambda i: (jnp.maximum(i * blocks_per_step - 1, 0), 0)),
        ],
        out_specs=pl.BlockSpec((rows, ATTN_Q_WIDTH), lambda i: (i, 0)),
        out_shape=jax.ShapeDtypeStruct((m, ATTN_Q_WIDTH), BF16),
        compiler_params=_params("parallel"),
        name="swa_attention",
    )(sinks, q, kv, kv)


def _dup_heads(a):
    lead = a.shape[:-1]
    a = a.reshape(*lead, ATTN_KV_HEADS, 1, ATTN_HEAD_DIM)
    a = jnp.broadcast_to(a, (*lead, ATTN_KV_HEADS, 2, ATTN_HEAD_DIM))
    return a.reshape(*lead, 2 * ATTN_KV_WIDTH)


def _ffn_kernel(x_ref, wg_ref, wv_ref, cp_ref, wo_ref, gain_ref, bias_ref, o_ref, xb_ref, tail_ref,
                *blocks, tiles_per_seq, sub_rows, sub_cols):
    n_blocks = (len(blocks) - FFN_STAGING_BUFFERS) // 2
    h_refs, act_refs = blocks[:n_blocks], blocks[n_blocks:2 * n_blocks]
    xs_refs = [blocks[2 * n_blocks + b % FFN_STAGING_BUFFERS] for b in range(n_blocks)]
    acc_ref = o_ref
    i = pl.program_id(0)
    j = pl.program_id(1)
    row_block, tf = act_refs[0].shape
    lane_tiles = xs_refs[0].shape[0]
    sub = F32_SUBLANES
    tiles = row_block // sub
    top = (CONV_WIDTH - 1) * sub

    def strided_rows(b, start, stride):
        return jnp.concatenate([xs_refs[b][c, pl.ds(start, sub, stride=stride), :]
                                for c in range(lane_tiles)], axis=1)

    def to_lane_tile_major(b, value):
        for c in range(lane_tiles):
            xs_refs[b][c, :, :] = value[:, c * LANES:(c + 1) * LANES]

    @pl.when(jnp.logical_and(i == 0, j == 0))
    def _():
        tail_ref[...] = jnp.zeros_like(tail_ref)

    def load_block(b):
        rows = slice(b * row_block, (b + 1) * row_block)
        to_lane_tile_major(b, x_ref[rows, :])
        x = jnp.concatenate([strided_rows(b, k, tiles) for k in range(tiles)], axis=0)
        xb_ref[rows, :] = x.astype(BF16)
        acc_ref[rows, :] = DEEPNORM_ALPHA * x

    def store_block(b):
        rows = slice(b * row_block, (b + 1) * row_block)
        to_lane_tile_major(b, _layer_norm(acc_ref[rows, :], gain_ref[...], bias_ref[...]))
        for m in range(tiles):
            start = sub * ((sub * m) % tiles) + (sub * m) // tiles
            o_ref[b * row_block + sub * m:b * row_block + sub * (m + 1), :] = strided_rows(b, start, sub)

    seq_start = i % tiles_per_seq == 0
    first_sublane = lax.broadcasted_iota(jnp.int32, (sub, tf), 0) == 0

    def in_proj(b, prev_last):
        xb = xb_ref[b * row_block:(b + 1) * row_block, :]
        last = []
        for half, w_ref in enumerate((wg_ref, wv_ref)):
            cols = slice(half * tf, (half + 1) * tf)
            h = jnp.dot(xb, w_ref[...], preferred_element_type=F32)
            h_refs[b][top:, cols] = h
            last.append(h[row_block - top:, :])
            for t in range(CONV_WIDTH - 1):
                rows = slice(t * sub, (t + 1) * sub)
                h_refs[b][rows, cols] = jnp.where(first_sublane,
                                                  pltpu.roll(prev_last[half][rows, :], 1, axis=0),
                                                  pltpu.roll(last[half][rows, :], 1, axis=0))
        return last

    def conv_act(b):
        h_ref = h_refs[b]
        for c0 in range(0, tf, sub_cols):
            cg = slice(c0, c0 + sub_cols)
            cv = slice(tf + c0, tf + c0 + sub_cols)
            for r0 in range(0, row_block, sub_rows):
                def conv(cols, p0):
                    return (cp_ref[p0 + 3:p0 + 4, cg]
                            + h_ref[top + r0 - 2 * sub:top + r0 - 2 * sub + sub_rows, cols] * cp_ref[p0:p0 + 1, cg]
                            + h_ref[top + r0 - sub:top + r0 - sub + sub_rows, cols] * cp_ref[p0 + 1:p0 + 2, cg]
                            + h_ref[top + r0:top + r0 + sub_rows, cols] * cp_ref[p0 + 2:p0 + 3, cg])
                gate = conv(cg, 0)
                val = conv(cv, CONV_WIDTH + 1)
                gelu = 0.5 * gate * (1.0 + lax.erf(gate * math.sqrt(0.5)))
                act_refs[b][r0:r0 + sub_rows, cg] = (gelu * val).astype(BF16)

    def out_proj(b):
        rows = slice(b * row_block, (b + 1) * row_block)
        acc_ref[rows, :] += jnp.dot(act_refs[b][...], wo_ref[...], preferred_element_type=F32)

    def chunk(first, last_chunk):
        tail = jnp.where(seq_start, 0.0, tail_ref[j])
        last = [tail[:, 0:tf], tail[:, tf:]]
        for step in range(n_blocks + 2):
            if step < n_blocks:
                if first:
                    load_block(step)
                last = in_proj(step, last)
            if 0 <= step - 1 < n_blocks:
                conv_act(step - 1)
            if 0 <= step - 2 < n_blocks:
                out_proj(step - 2)
                if last_chunk:
                    store_block(step - 2)
        tail_ref[j] = jnp.concatenate(last, axis=1)

    nf = pl.num_programs(1)
    pl.when(j == 0)(lambda: chunk(True, False))
    pl.when(jnp.logical_and(j > 0, j < nf - 1))(lambda: chunk(False, False))
    pl.when(j == nf - 1)(lambda: chunk(False, True))


FFN_CHUNK = 512


def _conv_params(conv_w, conv_b, tf):
    layers = conv_w.shape[0]
    p = jnp.concatenate([conv_w, conv_b[:, None, :]], axis=1)
    p = p.reshape(layers, CONV_WIDTH + 1, 2, D_FF // tf, tf)
    return p.transpose(0, 3, 2, 1, 4).reshape(layers, D_FF // tf, 2 * (CONV_WIDTH + 1), tf)


def _ffn(x, layer, w_in, conv_p, w_out, gain, bias, seq, tm=1024, tf=FFN_CHUNK,
         row_block=128, sub_rows=32, sub_cols=256):
    m, d = x.shape
    nf = D_FF // tf
    n_blocks = tm // row_block
    top = (CONV_WIDTH - 1) * F32_SUBLANES
    return pl.pallas_call(
        functools.partial(_ffn_kernel, tiles_per_seq=seq // tm, sub_rows=sub_rows, sub_cols=sub_cols),
        grid=(m // tm, nf),
        in_specs=[
            pl.BlockSpec((tm, d), lambda i, j: (i, 0)),
            pl.BlockSpec((None, d, tf), lambda i, j: (layer, 0, j)),
            pl.BlockSpec((None, d, tf), lambda i, j: (layer, 0, nf + j)),
            pl.BlockSpec((None, None, 2 * (CONV_WIDTH + 1), tf), lambda i, j: (layer, j, 0, 0)),
            pl.BlockSpec((None, tf, d), lambda i, j: (layer, j, 0)),
            pl.BlockSpec((1, d), lambda i, j: (0, 0)),
            pl.BlockSpec((1, d), lambda i, j: (0, 0)),
        ],
        out_specs=pl.BlockSpec((tm, d), lambda i, j: (i, 0)),
        out_shape=jax.ShapeDtypeStruct((m, d), F32),
        scratch_shapes=([pltpu.VMEM((tm, d), BF16),
                         pltpu.VMEM((nf, top, 2 * tf), F32)]
                        + [pltpu.VMEM((top + row_block, 2 * tf), F32)] * n_blocks
                        + [pltpu.VMEM((row_block, tf), BF16)] * n_blocks
                        + [pltpu.VMEM((d // LANES, row_block, LANES), F32)] * FFN_STAGING_BUFFERS),
        compiler_params=pltpu.CompilerParams(dimension_semantics=("arbitrary", "arbitrary"),
                                             vmem_limit_bytes=FFN_VMEM_LIMIT_BYTES),
        name="conv_glu_ffn",
    )(x, w_in, w_in, conv_p, w_out, gain, bias)


def kernel(x, ln_gain, ln_bias, even_w_in, pool_w, pool_scale, even_w_out, attn_w_qkv, attn_b_qkv, attn_sinks, attn_w_out, ffn_w_in, ffn_conv_w, ffn_conv_b, ffn_w_out):
    batch, seq, d = x.shape
    xf = x.reshape(batch * seq, d)
    row = lambda a: a.reshape(1, -1)
    even_w_in, even_w_out = even_w_in.astype(BF16), even_w_out.astype(BF16)
    attn_w_qkv, attn_w_out = attn_w_qkv.astype(BF16), attn_w_out.astype(BF16)
    ffn_w_in, ffn_w_out = ffn_w_in.astype(BF16), ffn_w_out.astype(BF16)
    ffn_conv_p = _conv_params(ffn_conv_w, ffn_conv_b, FFN_CHUNK)
    no_bias = jnp.zeros((1, even_w_in.shape[2]), F32)
    k_cols = slice(ATTN_Q_WIDTH, ATTN_Q_WIDTH + ATTN_KV_WIDTH)
    v_cols = slice(ATTN_Q_WIDTH + ATTN_KV_WIDTH, ATTN_Q_WIDTH + 2 * ATTN_KV_WIDTH)
    w_kv = jnp.concatenate([_dup_heads(attn_w_qkv[:, :, k_cols]), _dup_heads(attn_w_qkv[:, :, v_cols])], axis=2)
    for layer in range(DEPTH):
        li = layer // 2
        if layer % 2 == 0:
            qkv = _matmul(xf, even_w_in, li, no_bias, 0, 3 * RET_WIDTH, BF16)
            gu = _matmul(xf, even_w_in, li, no_bias, 3 * RET_WIDTH, RET_WIDTH + POOL_WIDTH, F32)
            y_ret = _retention(qkv, gu, batch, seq)
            y_pool = _pool(gu, pool_w[li].astype(BF16), row(pool_scale[li]), seq)
            xf = _proj_ln([y_ret, y_pool], even_w_out, li, xf,
                          row(ln_gain[layer, 0]), row(ln_bias[layer, 0]))
        else:
            b_qkv = row(attn_b_qkv[li])
            b_kv = jnp.concatenate([_dup_heads(b_qkv[:, k_cols]), _dup_heads(b_qkv[:, v_cols])], axis=1)
            q = _matmul(xf, attn_w_qkv, li, b_qkv, 0, ATTN_Q_WIDTH, BF16)
            kv = _matmul(xf, w_kv, li, b_kv, 0, 4 * ATTN_KV_WIDTH, BF16)
            o = _attention(q, kv, row(attn_sinks[li]), seq)
            xf = _proj_ln([o], attn_w_out, li, xf,
                          row(ln_gain[layer, 0]), row(ln_bias[layer, 0]))
        xf = _ffn(xf, layer, ffn_w_in, ffn_conv_p, ffn_w_out,
                  row(ln_gain[layer, 1]), row(ln_bias[layer, 1]), seq)
    return xf.reshape(batch, seq, d)
```

```python
import functools
import math

import numpy as np
import jax
import jax.numpy as jnp
from jax import lax
from jax.experimental import pallas as pl
from jax.experimental.pallas import tpu as pltpu

F32 = jnp.float32
BF16 = jnp.bfloat16

D_MODEL = 2048
DEPTH = 4
RET_HEAD_DIM = 256
RET_HEADS = 4
RET_WIDTH = RET_HEADS * RET_HEAD_DIM
RET_CHUNK = 128
POOL_WINDOWS = (2, 4, 8, 16)
POOL_GROUP = 256
POOL_WIDTH = len(POOL_WINDOWS) * POOL_GROUP
ATTN_HEAD_DIM = 64
ATTN_Q_HEADS = 32
ATTN_KV_HEADS = 4
ATTN_GROUP = ATTN_Q_HEADS // ATTN_KV_HEADS
ATTN_Q_WIDTH = ATTN_Q_HEADS * ATTN_HEAD_DIM
ATTN_KV_WIDTH = ATTN_KV_HEADS * ATTN_HEAD_DIM
WINDOW = 128
ATTN_BLOCK = 128
D_FF = 5632
CONV_WIDTH = 3
DEEPNORM_ALPHA = (2 * DEPTH) ** 0.25
LN_EPS = 1e-5

V7X_VMEM_BYTES = 64 * 1024 * 1024
VMEM_LIMIT_BYTES = V7X_VMEM_BYTES - 8 * 1024 * 1024
FFN_VMEM_LIMIT_BYTES = V7X_VMEM_BYTES - 3 * 1024 * 1024
FFN_STAGING_BUFFERS = 2
LANES = 128
F32_SUBLANES = 8
POOL_HALO = 16

_ALIBI_SLOPES = tuple(
    float(v) for v in (np.float32(2.0 ** (-8.0 / ATTN_Q_HEADS))
                       ** np.arange(1, ATTN_Q_HEADS + 1, dtype=np.float32)))


def _params(*semantics):
    return pltpu.CompilerParams(dimension_semantics=semantics,
                                vmem_limit_bytes=VMEM_LIMIT_BYTES)


def _layer_norm(z, gain, bias):
    mu = jnp.mean(z, axis=-1, keepdims=True)
    zc = z - mu
    var = jnp.mean(zc * zc, axis=-1, keepdims=True)
    return zc * lax.rsqrt(var + LN_EPS) * gain + bias


def _matmul_kernel(x_ref, w_ref, b_ref, o_ref, xb_ref):
    @pl.when(pl.program_id(1) == 0)
    def _():
        xb_ref[...] = x_ref[...].astype(BF16)

    acc = jnp.dot(xb_ref[...], w_ref[...], preferred_element_type=F32)
    o_ref[...] = (acc + b_ref[...]).astype(o_ref.dtype)


def _matmul(x, w, layer, b, col_start, n, out_dtype, tm=1024, tn=1024):
    m, k = x.shape
    off = col_start // tn
    assert col_start % tn == 0 and n % tn == 0
    return pl.pallas_call(
        _matmul_kernel,
        grid=(m // tm, n // tn),
        in_specs=[
            pl.BlockSpec((tm, k), lambda i, j: (i, 0)),
            pl.BlockSpec((None, k, tn), lambda i, j: (layer, 0, off + j)),
            pl.BlockSpec((1, tn), lambda i, j: (0, off + j)),
        ],
        out_specs=pl.BlockSpec((tm, tn), lambda i, j: (i, j)),
        out_shape=jax.ShapeDtypeStruct((m, n), out_dtype),
        scratch_shapes=[pltpu.VMEM((tm, k), BF16)],
        compiler_params=_params("parallel", "arbitrary"),
        name="proj_matmul",
    )(x, w, b)


def _retention_kernel(q_ref, k_ref, v_ref, g_ref, o_ref, state_ref, *, chunks_per_step):
    c_len, dh = RET_CHUNK, RET_HEAD_DIM

    @pl.when(pl.program_id(1) == 0)
    def _():
        state_ref[...] = jnp.zeros_like(state_ref)

    row = lax.broadcasted_iota(jnp.int32, (c_len, c_len), 0)
    col = lax.broadcasted_iota(jnp.int32, (c_len, c_len), 1)
    diff = (row - col).astype(F32)
    pos = lax.broadcasted_iota(jnp.int32, (c_len, dh), 0).astype(F32)
    k_scale = dh ** -0.5
    for h in range(RET_HEADS):
        log_gamma = math.log1p(-(2.0 ** (-5.0 - h)))
        decay = jnp.where(diff >= 0, jnp.exp(log_gamma * jnp.maximum(diff, 0.0)), 0.0) * k_scale
        zeta = jnp.exp(log_gamma * (c_len - 1.0 - pos)) * k_scale
        xi = jnp.exp(log_gamma * (pos + 1.0))
        chunk_decay = math.exp(log_gamma * c_len)
        cols = slice(h * dh, (h + 1) * dh)
        for c in range(chunks_per_step):
            rows = slice(c * c_len, (c + 1) * c_len)
            q = q_ref[rows, cols]
            k = k_ref[rows, cols]
            v = v_ref[rows, cols]
            state = state_ref[h]
            scores = lax.dot_general(q, k, (((1,), (1,)), ((), ())),
                                     preferred_element_type=F32) * decay
            inner = jnp.dot(scores.astype(BF16), v, preferred_element_type=F32)
            cross = jnp.dot(q, state.astype(BF16), preferred_element_type=F32) * xi
            kz = (k.astype(F32) * zeta).astype(BF16)
            kv = lax.dot_general(kz, v, (((0,), (0,)), ((), ())),
                                 preferred_element_type=F32)
            state_ref[h] = state * chunk_decay + kv
            y = inner + cross
            mu = jnp.mean(y, axis=-1, keepdims=True)
            yc = y - mu
            var = jnp.mean(yc * yc, axis=-1, keepdims=True)
            yn = yc * lax.rsqrt(var + LN_EPS)
            g = g_ref[rows, cols]
            o_ref[rows, cols] = (g * jax.nn.sigmoid(g) * yn).astype(o_ref.dtype)


def _retention(qkv, gu, batch, seq, chunks_per_step=8):
    m = qkv.shape[0]
    tr = chunks_per_step * RET_CHUNK
    steps = seq // tr
    row_map = lambda col: (lambda b, n: (b * steps + n, col))
    return pl.pallas_call(
        functools.partial(_retention_kernel, chunks_per_step=chunks_per_step),
        grid=(batch, steps),
        in_specs=[
            pl.BlockSpec((tr, RET_WIDTH), row_map(0)),
            pl.BlockSpec((tr, RET_WIDTH), row_map(1)),
            pl.BlockSpec((tr, RET_WIDTH), row_map(2)),
            pl.BlockSpec((tr, RET_WIDTH), row_map(0)),
        ],
        out_specs=pl.BlockSpec((tr, RET_WIDTH), row_map(0)),
        out_shape=jax.ShapeDtypeStruct((m, RET_WIDTH), BF16),
        scratch_shapes=[pltpu.VMEM((RET_HEADS, RET_HEAD_DIM, RET_HEAD_DIM), F32)],
        compiler_params=_params("parallel", "arbitrary"),
        name="retention",
    )(qkv, qkv, qkv, gu)


def _pool_kernel(u_ref, halo_ref, w_ref, scale_ref, o_ref, *, tiles_per_seq):
    tr = u_ref.shape[0]
    tile_in_seq = pl.program_id(0) % tiles_per_seq
    halo_on = (tile_in_seq > 0).astype(F32)
    t = tile_in_seq * tr + lax.broadcasted_iota(jnp.int32, (tr, POOL_GROUP), 0)
    for gi, win in enumerate(POOL_WINDOWS):
        cols = slice(gi * POOL_GROUP, (gi + 1) * POOL_GROUP)
        u = u_ref[:, cols]
        ext = jnp.concatenate([halo_ref[:, cols] * halo_on, u], axis=0)
        span = 1
        while span < win:
            ext = ext + pltpu.roll(ext, span, axis=0)
            span *= 2
        count = jnp.minimum(t + 1, win).astype(F32)
        pooled = ext[POOL_HALO:, :] / count - u
        y = jnp.dot(pooled.astype(BF16), w_ref[gi], preferred_element_type=F32)
        o_ref[:, cols] = (y * scale_ref[:, cols]).astype(o_ref.dtype)


def _pool(gu, pool_w, pool_scale, seq, tr=512):
    m = gu.shape[0]
    halo_blocks = tr // POOL_HALO
    return pl.pallas_call(
        functools.partial(_pool_kernel, tiles_per_seq=seq // tr),
        grid=(m // tr,),
        in_specs=[
            pl.BlockSpec((tr, POOL_WIDTH), lambda i: (i, 1)),
            pl.BlockSpec((POOL_HALO, POOL_WIDTH),
                         lambda i: (jnp.maximum(i * halo_blocks - 1, 0), 1)),
            pl.BlockSpec((len(POOL_WINDOWS), POOL_GROUP, POOL_GROUP), lambda i: (0, 0, 0)),
            pl.BlockSpec((1, POOL_WIDTH), lambda i: (0, 0)),
        ],
        out_specs=pl.BlockSpec((tr, POOL_WIDTH), lambda i: (i, 0)),
        out_shape=jax.ShapeDtypeStruct((m, POOL_WIDTH), BF16),
        compiler_params=_params("parallel"),
        name="pool",
    )(gu, gu, pool_w, pool_scale)


def _proj_ln_kernel(*refs, n_in, row_block):
    y_refs, w_refs = refs[:n_in], refs[n_in:2 * n_in]
    x_ref, gain_ref, bias_ref, o_ref = refs[2 * n_in:]
    for r0 in range(0, x_ref.shape[0], row_block):
        rows = slice(r0, r0 + row_block)
        acc = jnp.dot(y_refs[0][rows, :], w_refs[0][...], preferred_element_type=F32)
        for y_ref, w_ref in zip(y_refs[1:], w_refs[1:]):
            acc = acc + jnp.dot(y_ref[rows, :], w_ref[...], preferred_element_type=F32)
        z = DEEPNORM_ALPHA * x_ref[rows, :] + acc
        o_ref[rows, :] = _layer_norm(z, gain_ref[...], bias_ref[...])


def _proj_ln(ys, w, layer, x, gain, bias, tm=1024, row_block=128):
    m, d = x.shape
    n_in = len(ys)
    width = ys[0].shape[1]
    assert all(y.shape[1] == width for y in ys) and n_in * width == w.shape[1]
    in_specs = [pl.BlockSpec((tm, width), lambda i: (i, 0)) for _ in ys]
    in_specs += [pl.BlockSpec((None, width, d), lambda i, s=s: (layer, s, 0), pipeline_mode=pl.Buffered(1))
                 for s in range(n_in)]
    in_specs += [
        pl.BlockSpec((tm, d), lambda i: (i, 0)),
        pl.BlockSpec((1, d), lambda i: (0, 0)),
        pl.BlockSpec((1, d), lambda i: (0, 0)),
    ]
    return pl.pallas_call(
        functools.partial(_proj_ln_kernel, n_in=n_in, row_block=row_block),
        grid=(m // tm,),
        in_specs=in_specs,
        out_specs=pl.BlockSpec((tm, d), lambda i: (i, 0)),
        out_shape=jax.ShapeDtypeStruct((m, d), F32),
        compiler_params=_params("parallel"),
        name="proj_ln",
    )(*ys, *([w] * n_in), x, gain, bias)


def _attn_kernel(sink_ref, q_ref, kv_cur_ref, kv_prev_ref, o_ref, *, steps_per_seq):
    blk, dh = ATTN_BLOCK, ATTN_HEAD_DIM
    lanes = 2 * dh
    pairs = ATTN_GROUP // 2
    i = lax.broadcasted_iota(jnp.int32, (blk, blk), 0)
    j = lax.broadcasted_iota(jnp.int32, (blk, blk), 1)
    upper = j > i
    delta_f = jnp.where(upper, blk + i - j, i - j).astype(F32)
    lane = lax.broadcasted_iota(jnp.int32, (2 * blk, lanes), 1)
    low = lane < dh
    for n in range(q_ref.shape[0] // blk):
        rows = slice(n * blk, (n + 1) * blk)
        if n == 0:
            prev_ok = upper & (pl.program_id(0) % steps_per_seq > 0)
            prev_ref, prev_rows = kv_prev_ref, slice(0, blk)
        else:
            prev_ok = upper
            prev_ref, prev_rows = kv_cur_ref, slice((n - 1) * blk, n * blk)
        _attn_block(sink_ref, q_ref, o_ref, rows, prev_ref, prev_rows, kv_cur_ref,
                    upper, prev_ok, delta_f, low)


def _attn_block(sink_ref, q_ref, o_ref, rows, prev_ref, prev_rows, cur_ref, upper, prev_ok, delta_f, low):
    blk, dh = ATTN_BLOCK, ATTN_HEAD_DIM
    lanes = 2 * dh
    pairs = ATTN_GROUP // 2

    def block_diag(tile):
        cols = slice(tile * lanes, (tile + 1) * lanes)
        t = jnp.concatenate([prev_ref[prev_rows, cols], cur_ref[rows, cols]], axis=0)
        zero = jnp.zeros_like(t)
        return jnp.concatenate([jnp.where(low, t, zero), jnp.where(low, zero, t)], axis=0)

    def scores(kh):
        qs = jnp.concatenate([q_ref[rows, (kh * pairs + p) * lanes:(kh * pairs + p + 1) * lanes]
                              for p in range(pairs)], axis=0)
        k_bd = block_diag(kh)
        return lax.dot_general(qs, k_bd, (((1,), (1,)), ((), ())), preferred_element_type=F32)

    def softmax(kh, s_all):
        rows_out = []
        for p in range(pairs):
            parts = []
            for e in range(2):
                h = kh * ATTN_GROUP + 2 * p + e
                blk_s = s_all[p * blk:(p + 1) * blk, e * 2 * blk:(e + 1) * 2 * blk]
                s_prev = blk_s[:, :blk]
                if prev_ok is not upper:
                    s_prev = jnp.where(prev_ok, s_prev, -jnp.inf)
                s = jnp.where(upper, s_prev, blk_s[:, blk:])
                s = s * dh ** -0.5 - _ALIBI_SLOPES[h] * delta_f
                sink = sink_ref[0, h]
                mx = jnp.maximum(jnp.max(s, axis=-1, keepdims=True), sink)
                pr = jnp.exp(s - mx)
                denom = jnp.sum(pr, axis=-1, keepdims=True) + jnp.exp(sink - mx)
                probs = (pr / denom).astype(BF16)
                zero = jnp.zeros_like(probs)
                parts += [jnp.where(upper, probs, zero), jnp.where(upper, zero, probs)]
            rows_out.append(jnp.concatenate(parts, axis=1))
        return jnp.concatenate(rows_out, axis=0)

    def values(kh, probs):
        v_bd = block_diag(ATTN_KV_HEADS + kh)
        o = jnp.dot(probs, v_bd, preferred_element_type=F32)
        for p in range(pairs):
            cols = slice((kh * pairs + p) * lanes, (kh * pairs + p + 1) * lanes)
            o_ref[rows, cols] = o[p * blk:(p + 1) * blk, :].astype(o_ref.dtype)

    s_next = scores(0)
    for kh in range(ATTN_KV_HEADS):
        s_cur = s_next
        if kh + 1 < ATTN_KV_HEADS:
            s_next = scores(kh + 1)
        values(kh, softmax(kh, s_cur))


def _attention(q, kv, sinks, seq, blocks_per_step=4):
    m = q.shape[0]
    blk = ATTN_BLOCK
    rows = blocks_per_step * blk
    kvw = kv.shape[1]
    return pl.pallas_call(
        functools.partial(_attn_kernel, steps_per_seq=seq // rows),
        grid=(m // rows,),
        in_specs=[
            pl.BlockSpec(memory_space=pltpu.SMEM),
            pl.BlockSpec((rows, ATTN_Q_WIDTH), lambda i: (i, 0)),
            pl.BlockSpec((rows, kvw), lambda i: (i, 0)),
            pl.BlockSpec((blk, kvw), lambda i: (jnp.maximum(i * blocks_per_step - 1, 0), 0)),
        ],
        out_specs=pl.BlockSpec((rows, ATTN_Q_WIDTH), lambda i: (i, 0)),
        out_shape=jax.ShapeDtypeStruct((m, ATTN_Q_WIDTH), BF16),
        compiler_params=_params("parallel"),
        name="swa_attention",
    )(sinks, q, kv, kv)


def _dup_heads(a):
    lead = a.shape[:-1]
    a = a.reshape(*lead, ATTN_KV_HEADS, 1, ATTN_HEAD_DIM)
    a = jnp.broadcast_to(a, (*lead, ATTN_KV_HEADS, 2, ATTN_HEAD_DIM))
    return a.reshape(*lead, 2 * ATTN_KV_WIDTH)


def _ffn_kernel(x_ref, wg_ref, wv_ref, cp_ref, wo_ref, gain_ref, bias_ref, o_ref, xb_ref, tail_ref,
                *blocks, tiles_per_seq, sub_rows, sub_cols):
    n_blocks = (len(blocks) - FFN_STAGING_BUFFERS) // 2
    h_refs, act_refs = blocks[:n_blocks], blocks[n_blocks:2 * n_blocks]
    xs_refs = [blocks[2 * n_blocks + b % FFN_STAGING_BUFFERS] for b in range(n_blocks)]
    acc_ref = o_ref
    i = pl.program_id(0)
    j = pl.program_id(1)
    row_block, tf = act_refs[0].shape
    lane_tiles = xs_refs[0].shape[0]
    sub = F32_SUBLANES
    tiles = row_block // sub
    top = (CONV_WIDTH - 1) * sub

    def strided_rows(b, start, stride):
        return jnp.concatenate([xs_refs[b][c, pl.ds(start, sub, stride=stride), :]
                                for c in range(lane_tiles)], axis=1)

    def to_lane_tile_major(b, value):
        for c in range(lane_tiles):
            xs_refs[b][c, :, :] = value[:, c * LANES:(c + 1) * LANES]

    @pl.when(jnp.logical_and(i == 0, j == 0))
    def _():
        tail_ref[...] = jnp.zeros_like(tail_ref)

    def load_block(b):
        rows = slice(b * row_block, (b + 1) * row_block)
        to_lane_tile_major(b, x_ref[rows, :])
        x = jnp.concatenate([strided_rows(b, k, tiles) for k in range(tiles)], axis=0)
        xb_ref[rows, :] = x.astype(BF16)
        acc_ref[rows, :] = DEEPNORM_ALPHA * x

    def store_block(b):
        rows = slice(b * row_block, (b + 1) * row_block)
        to_lane_tile_major(b, _layer_norm(acc_ref[rows, :], gain_ref[...], bias_ref[...]))
        for m in range(tiles):
            start = sub * ((sub * m) % tiles) + (sub * m) // tiles
            o_ref[b * row_block + sub * m:b * row_block + sub * (m + 1), :] = strided_rows(b, start, sub)

    seq_start = i % tiles_per_seq == 0
    first_sublane = lax.broadcasted_iota(jnp.int32, (sub, tf), 0) == 0

    def in_proj(b, prev_last):
        xb = xb_ref[b * row_block:(b + 1) * row_block, :]
        last = []
        for half, w_ref in enumerate((wg_ref, wv_ref)):
            cols = slice(half * tf, (half + 1) * tf)
            h = jnp.dot(xb, w_ref[...], preferred_element_type=F32)
            h_refs[b][top:, cols] = h
            last.append(h[row_block - top:, :])
            for t in range(CONV_WIDTH - 1):
                rows = slice(t * sub, (t + 1) * sub)
                h_refs[b][rows, cols] = jnp.where(first_sublane,
                                                  pltpu.roll(prev_last[half][rows, :], 1, axis=0),
                                                  pltpu.roll(last[half][rows, :], 1, axis=0))
        return last

    def conv_act(b):
        h_ref = h_refs[b]
        for c0 in range(0, tf, sub_cols):
            cg = slice(c0, c0 + sub_cols)
            cv = slice(tf + c0, tf + c0 + sub_cols)
            for r0 in range(0, row_block, sub_rows):
                def conv(cols, p0):
                    return (cp_ref[p0 + 3:p0 + 4, cg]
                            + h_ref[top + r0 - 2 * sub:top + r0 - 2 * sub + sub_rows, cols] * cp_ref[p0:p0 + 1, cg]
                            + h_ref[top + r0 - sub:top + r0 - sub + sub_rows, cols] * cp_ref[p0 + 1:p0 + 2, cg]
                            + h_ref[top + r0:top + r0 + sub_rows, cols] * cp_ref[p0 + 2:p0 + 3, cg])
                gate = conv(cg, 0)
                val = conv(cv, CONV_WIDTH + 1)
                gelu = 0.5 * gate * (1.0 + lax.erf(gate * math.sqrt(0.5)))
                act_refs[b][r0:r0 + sub_rows, cg] = (gelu * val).astype(BF16)

    def out_proj(b):
        rows = slice(b * row_block, (b + 1) * row_block)
        acc_ref[rows, :] += jnp.dot(act_refs[b][...], wo_ref[...], preferred_element_type=F32)

    def chunk(first, last_chunk):
        tail = jnp.where(seq_start, 0.0, tail_ref[j])
        last = [tail[:, 0:tf], tail[:, tf:]]
        for step in range(n_blocks + 2):
            if step < n_blocks:
                if first:
                    load_block(step)
                last = in_proj(step, last)
            if 0 <= step - 1 < n_blocks:
                conv_act(step - 1)
            if 0 <= step - 2 < n_blocks:
                out_proj(step - 2)
                if last_chunk:
                    store_block(step - 2)
        tail_ref[j] = jnp.concatenate(last, axis=1)

    nf = pl.num_programs(1)
    pl.when(j == 0)(lambda: chunk(True, False))
    pl.when(jnp.logical_and(j > 0, j < nf - 1))(lambda: chunk(False, False))
    pl.when(j == nf - 1)(lambda: chunk(False, True))


FFN_CHUNK = 512


def _conv_params(conv_w, conv_b, tf):
    layers = conv_w.shape[0]
    p = jnp.concatenate([conv_w, conv_b[:, None, :]], axis=1)
    p = p.reshape(layers, CONV_WIDTH + 1, 2, D_FF // tf, tf)
    return p.transpose(0, 3, 2, 1, 4).reshape(layers, D_FF // tf, 2 * (CONV_WIDTH + 1), tf)


def _ffn(x, layer, w_in, conv_p, w_out, gain, bias, seq, tm=1024, tf=FFN_CHUNK,
         row_block=128, sub_rows=32, sub_cols=256):
    m, d = x.shape
    nf = D_FF // tf
    n_blocks = tm // row_block
    top = (CONV_WIDTH - 1) * F32_SUBLANES
    return pl.pallas_call(
        functools.partial(_ffn_kernel, tiles_per_seq=seq // tm, sub_rows=sub_rows, sub_cols=sub_cols),
        grid=(m // tm, nf),
        in_specs=[
            pl.BlockSpec((tm, d), lambda i, j: (i, 0)),
            pl.BlockSpec((None, d, tf), lambda i, j: (layer, 0, j)),
            pl.BlockSpec((None, d, tf), lambda i, j: (layer, 0, nf + j)),
            pl.BlockSpec((None, None, 2 * (CONV_WIDTH + 1), tf), lambda i, j: (layer, j, 0, 0)),
            pl.BlockSpec((None, tf, d), lambda i, j: (layer, j, 0)),
            pl.BlockSpec((1, d), lambda i, j: (0, 0)),
            pl.BlockSpec((1, d), lambda i, j: (0, 0)),
        ],
        out_specs=pl.BlockSpec((tm, d), lambda i, j: (i, 0)),
        out_shape=jax.ShapeDtypeStruct((m, d), F32),
        scratch_shapes=([pltpu.VMEM((tm, d), BF16),
                         pltpu.VMEM((nf, top, 2 * tf), F32)]
                        + [pltpu.VMEM((top + row_block, 2 * tf), F32)] * n_blocks
                        + [pltpu.VMEM((row_block, tf), BF16)] * n_blocks
                        + [pltpu.VMEM((d // LANES, row_block, LANES), F32)] * FFN_STAGING_BUFFERS),
        compiler_params=pltpu.CompilerParams(dimension_semantics=("arbitrary", "arbitrary"),
                                             vmem_limit_bytes=FFN_VMEM_LIMIT_BYTES),
        name="conv_glu_ffn",
    )(x, w_in, w_in, conv_p, w_out, gain, bias)


def kernel(x, ln_gain, ln_bias, even_w_in, pool_w, pool_scale, even_w_out, attn_w_qkv, attn_b_qkv, attn_sinks, attn_w_out, ffn_w_in, ffn_conv_w, ffn_conv_b, ffn_w_out):
    batch, seq, d = x.shape
    xf = x.reshape(batch * seq, d)
    row = lambda a: a.reshape(1, -1)
    even_w_in, even_w_out = even_w_in.astype(BF16), even_w_out.astype(BF16)
    attn_w_qkv, attn_w_out = attn_w_qkv.astype(BF16), attn_w_out.astype(BF16)
    ffn_w_in, ffn_w_out = ffn_w_in.astype(BF16), ffn_w_out.astype(BF16)
    ffn_conv_p = _conv_params(ffn_conv_w, ffn_conv_b, FFN_CHUNK)
    no_bias = jnp.zeros((1, even_w_in.shape[2]), F32)
    k_cols = slice(ATTN_Q_WIDTH, ATTN_Q_WIDTH + ATTN_KV_WIDTH)
    v_cols = slice(ATTN_Q_WIDTH + ATTN_KV_WIDTH, ATTN_Q_WIDTH + 2 * ATTN_KV_WIDTH)
    w_kv = jnp.concatenate([_dup_heads(attn_w_qkv[:, :, k_cols]), _dup_heads(attn_w_qkv[:, :, v_cols])], axis=2)
    for layer in range(DEPTH):
        li = layer // 2
        if layer % 2 == 0:
            qkv = _matmul(xf, even_w_in, li, no_bias, 0, 3 * RET_WIDTH, BF16)
            gu = _matmul(xf, even_w_in, li, no_bias, 3 * RET_WIDTH, RET_WIDTH + POOL_WIDTH, F32)
            y_ret = _retention(qkv, gu, batch, seq)
            y_pool = _pool(gu, pool_w[li].astype(BF16), row(pool_scale[li]), seq)
            xf = _proj_ln([y_ret, y_pool], even_w_out, li, xf,
                          row(ln_gain[layer, 0]), row(ln_bias[layer, 0]))
        else:
            b_qkv = row(attn_b_qkv[li])
            b_kv = jnp.concatenate([_dup_heads(b_qkv[:, k_cols]), _dup_heads(b_qkv[:, v_cols])], axis=1)
            q = _matmul(xf, attn_w_qkv, li, b_qkv, 0, ATTN_Q_WIDTH, BF16)
            kv = _matmul(xf, w_kv, li, b_kv, 0, 4 * ATTN_KV_WIDTH, BF16)
            o = _attention(q, kv, row(attn_sinks[li]), seq)
            xf = _proj_ln([o], attn_w_out, li, xf,
                          row(ln_gain[layer, 0]), row(ln_bias[layer, 0]))
        xf = _ffn(xf, layer, ffn_w_in, ffn_conv_p, ffn_w_out,
                  row(ln_gain[layer, 1]), row(ln_bias[layer, 1]), seq)
    return xf.reshape(batch, seq, d)
```

```python
import functools
import math

import numpy as np
import jax
import jax.numpy as jnp
from jax import lax
from jax.experimental import pallas as pl
from jax.experimental.pallas import tpu as pltpu

F32 = jnp.float32
BF16 = jnp.bfloat16

D_MODEL = 2048
DEPTH = 4
RET_HEAD_DIM = 256
RET_HEADS = 4
RET_WIDTH = RET_HEADS * RET_HEAD_DIM
RET_CHUNK = 128
POOL_WINDOWS = (2, 4, 8, 16)
POOL_GROUP = 256
POOL_WIDTH = len(POOL_WINDOWS) * POOL_GROUP
ATTN_HEAD_DIM = 64
ATTN_Q_HEADS = 32
ATTN_KV_HEADS = 4
ATTN_GROUP = ATTN_Q_HEADS // ATTN_KV_HEADS
ATTN_Q_WIDTH = ATTN_Q_HEADS * ATTN_HEAD_DIM
ATTN_KV_WIDTH = ATTN_KV_HEADS * ATTN_HEAD_DIM
WINDOW = 128
ATTN_BLOCK = 128
D_FF = 5632
CONV_WIDTH = 3
DEEPNORM_ALPHA = (2 * DEPTH) ** 0.25
LN_EPS = 1e-5

V7X_VMEM_BYTES = 64 * 1024 * 1024
VMEM_LIMIT_BYTES = V7X_VMEM_BYTES - 8 * 1024 * 1024
FFN_VMEM_LIMIT_BYTES = V7X_VMEM_BYTES - 3 * 1024 * 1024
FFN_STAGING_BUFFERS = 2
LANES = 128
F32_SUBLANES = 8
POOL_HALO = 16

_ALIBI_SLOPES = tuple(
    float(v) for v in (np.float32(2.0 ** (-8.0 / ATTN_Q_HEADS))
                       ** np.arange(1, ATTN_Q_HEADS + 1, dtype=np.float32)))


def _params(*semantics):
    return pltpu.CompilerParams(dimension_semantics=semantics,
                                vmem_limit_bytes=VMEM_LIMIT_BYTES)


def _layer_norm(z, gain, bias):
    mu = jnp.mean(z, axis=-1, keepdims=True)
    zc = z - mu
    var = jnp.mean(zc * zc, axis=-1, keepdims=True)
    return zc * lax.rsqrt(var + LN_EPS) * gain + bias


def _matmul_kernel(x_ref, w_ref, b_ref, o_ref, xb_ref):
    @pl.when(pl.program_id(1) == 0)
    def _():
        xb_ref[...] = x_ref[...].astype(BF16)

    acc = jnp.dot(xb_ref[...], w_ref[...], preferred_element_type=F32)
    o_ref[...] = (acc + b_ref[...]).astype(o_ref.dtype)


def _matmul(x, w, layer, b, col_start, n, out_dtype, tm=1024, tn=1024):
    m, k = x.shape
    off = col_start // tn
    assert col_start % tn == 0 and n % tn == 0
    return pl.pallas_call(
        _matmul_kernel,
        grid=(m // tm, n // tn),
        in_specs=[
            pl.BlockSpec((tm, k), lambda i, j: (i, 0)),
            pl.BlockSpec((None, k, tn), lambda i, j: (layer, 0, off + j)),
            pl.BlockSpec((1, tn), lambda i, j: (0, off + j)),
        ],
        out_specs=pl.BlockSpec((tm, tn), lambda i, j: (i, j)),
        out_shape=jax.ShapeDtypeStruct((m, n), out_dtype),
        scratch_shapes=[pltpu.VMEM((tm, k), BF16)],
        compiler_params=_params("parallel", "arbitrary"),
        name="proj_matmul",
    )(x, w, b)


def _retention_kernel(q_ref, k_ref, v_ref, g_ref, o_ref, state_ref, *, chunks_per_step):
    c_len, dh = RET_CHUNK, RET_HEAD_DIM

    @pl.when(pl.program_id(1) == 0)
    def _():
        state_ref[...] = jnp.zeros_like(state_ref)

    row = lax.broadcasted_iota(jnp.int32, (c_len, c_len), 0)
    col = lax.broadcasted_iota(jnp.int32, (c_len, c_len), 1)
    diff = (row - col).astype(F32)
    pos = lax.broadcasted_iota(jnp.int32, (c_len, dh), 0).astype(F32)
    k_scale = dh ** -0.5
    for h in range(RET_HEADS):
        log_gamma = math.log1p(-(2.0 ** (-5.0 - h)))
        decay = jnp.where(diff >= 0, jnp.exp(log_gamma * jnp.maximum(diff, 0.0)), 0.0) * k_scale
        zeta = jnp.exp(log_gamma * (c_len - 1.0 - pos)) * k_scale
        xi = jnp.exp(log_gamma * (pos + 1.0))
        chunk_decay = math.exp(log_gamma * c_len)
        cols = slice(h * dh, (h + 1) * dh)
        for c in range(chunks_per_step):
            rows = slice(c * c_len, (c + 1) * c_len)
            q = q_ref[rows, cols]
            k = k_ref[rows, cols]
            v = v_ref[rows, cols]
            state = state_ref[h]
            scores = lax.dot_general(q, k, (((1,), (1,)), ((), ())),
                                     preferred_element_type=F32) * decay
            inner = jnp.dot(scores.astype(BF16), v, preferred_element_type=F32)
            cross = jnp.dot(q, state.astype(BF16), preferred_element_type=F32) * xi
            kz = (k.astype(F32) * zeta).astype(BF16)
            kv = lax.dot_general(kz, v, (((0,), (0,)), ((), ())),
                                 preferred_element_type=F32)
            state_ref[h] = state * chunk_decay + kv
            y = inner + cross
            mu = jnp.mean(y, axis=-1, keepdims=True)
            yc = y - mu
            var = jnp.mean(yc * yc, axis=-1, keepdims=True)
            yn = yc * lax.rsqrt(var + LN_EPS)
            g = g_ref[rows, cols]
            o_ref[rows, cols] = (g * jax.nn.sigmoid(g) * yn).astype(o_ref.dtype)


def _retention(qkv, gu, batch, seq, chunks_per_step=8):
    m = qkv.shape[0]
    tr = chunks_per_step * RET_CHUNK
    steps = seq // tr
    row_map = lambda col: (lambda b, n: (b * steps + n, col))
    return pl.pallas_call(
        functools.partial(_retention_kernel, chunks_per_step=chunks_per_step),
        grid=(batch, steps),
        in_specs=[
            pl.BlockSpec((tr, RET_WIDTH), row_map(0)),
            pl.BlockSpec((tr, RET_WIDTH), row_map(1)),
            pl.BlockSpec((tr, RET_WIDTH), row_map(2)),
            pl.BlockSpec((tr, RET_WIDTH), row_map(0)),
        ],
        out_specs=pl.BlockSpec((tr, RET_WIDTH), row_map(0)),
        out_shape=jax.ShapeDtypeStruct((m, RET_WIDTH), BF16),
        scratch_shapes=[pltpu.VMEM((RET_HEADS, RET_HEAD_DIM, RET_HEAD_DIM), F32)],
        compiler_params=_params("parallel", "arbitrary"),
        name="retention",
    )(qkv, qkv, qkv, gu)


def _pool_kernel(u_ref, halo_ref, w_ref, scale_ref, o_ref, *, tiles_per_seq):
    tr = u_ref.shape[0]
    tile_in_seq = pl.program_id(0) % tiles_per_seq
    halo_on = (tile_in_seq > 0).astype(F32)
    t = tile_in_seq * tr + lax.broadcasted_iota(jnp.int32, (tr, POOL_GROUP), 0)
    for gi, win in enumerate(POOL_WINDOWS):
        cols = slice(gi * POOL_GROUP, (gi + 1) * POOL_GROUP)
        u = u_ref[:, cols]
        ext = jnp.concatenate([halo_ref[:, cols] * halo_on, u], axis=0)
        span = 1
        while span < win:
            ext = ext + pltpu.roll(ext, span, axis=0)
            span *= 2
        count = jnp.minimum(t + 1, win).astype(F32)
        pooled = ext[POOL_HALO:, :] / count - u
        y = jnp.dot(pooled.astype(BF16), w_ref[gi], preferred_element_type=F32)
        o_ref[:, cols] = (y * scale_ref[:, cols]).astype(o_ref.dtype)


def _pool(gu, pool_w, pool_scale, seq, tr=512):
    m = gu.shape[0]
    halo_blocks = tr // POOL_HALO
    return pl.pallas_call(
        functools.partial(_pool_kernel, tiles_per_seq=seq // tr),
        grid=(m // tr,),
        in_specs=[
            pl.BlockSpec((tr, POOL_WIDTH), lambda i: (i, 1)),
            pl.BlockSpec((POOL_HALO, POOL_WIDTH),
                         lambda i: (jnp.maximum(i * halo_blocks - 1, 0), 1)),
            pl.BlockSpec((len(POOL_WINDOWS), POOL_GROUP, POOL_GROUP), lambda i: (0, 0, 0)),
            pl.BlockSpec((1, POOL_WIDTH), lambda i: (0, 0)),
        ],
        out_specs=pl.BlockSpec((tr, POOL_WIDTH), lambda i: (i, 0)),
        out_shape=jax.ShapeDtypeStruct((m, POOL_WIDTH), BF16),
        compiler_params=_params("parallel"),
        name="pool",
    )(gu, gu, pool_w, pool_scale)


def _proj_ln_kernel(*refs, n_in, row_block):
    y_refs, w_refs = refs[:n_in], refs[n_in:2 * n_in]
    x_ref, gain_ref, bias_ref, o_ref = refs[2 * n_in:]
    for r0 in range(0, x_ref.shape[0], row_block):
        rows = slice(r0, r0 + row_block)
        acc = jnp.dot(y_refs[0][rows, :], w_refs[0][...], preferred_element_type=F32)
        for y_ref, w_ref in zip(y_refs[1:], w_refs[1:]):
            acc = acc + jnp.dot(y_ref[rows, :], w_ref[...], preferred_element_type=F32)
        z = DEEPNORM_ALPHA * x_ref[rows, :] + acc
        o_ref[rows, :] = _layer_norm(z, gain_ref[...], bias_ref[...])


def _proj_ln(ys, w, layer, x, gain, bias, tm=1024, row_block=128):
    m, d = x.shape
    n_in = len(ys)
    width = ys[0].shape[1]
    assert all(y.shape[1] == width for y in ys) and n_in * width == w.shape[1]
    in_specs = [pl.BlockSpec((tm, width), lambda i: (i, 0)) for _ in ys]
    in_specs += [pl.BlockSpec((None, width, d), lambda i, s=s: (layer, s, 0), pipeline_mode=pl.Buffered(1))
                 for s in range(n_in)]
    in_specs += [
        pl.BlockSpec((tm, d), lambda i: (i, 0)),
        pl.BlockSpec((1, d), lambda i: (0, 0)),
        pl.BlockSpec((1, d), lambda i: (0, 0)),
    ]
    return pl.pallas_call(
        functools.partial(_proj_ln_kernel, n_in=n_in, row_block=row_block),
        grid=(m // tm,),
        in_specs=in_specs,
        out_specs=pl.BlockSpec((tm, d), lambda i: (i, 0)),
        out_shape=jax.ShapeDtypeStruct((m, d), F32),
        compiler_params=_params("parallel"),
        name="proj_ln",
    )(*ys, *([w] * n_in), x, gain, bias)


def _attn_kernel(sink_ref, q_ref, kv_cur_ref, kv_prev_ref, o_ref, *, steps_per_seq):
    blk, dh = ATTN_BLOCK, ATTN_HEAD_DIM
    lanes = 2 * dh
    pairs = ATTN_GROUP // 2
    i = lax.broadcasted_iota(jnp.int32, (blk, blk), 0)
    j = lax.broadcasted_iota(jnp.int32, (blk, blk), 1)
    upper = j > i
    delta_f = jnp.where(upper, blk + i - j, i - j).astype(F32)
    lane = lax.broadcasted_iota(jnp.int32, (2 * blk, lanes), 1)
    low = lane < dh
    for n in range(q_ref.shape[0] // blk):
        rows = slice(n * blk, (n + 1) * blk)
        if n == 0:
            prev_ok = upper & (pl.program_id(0) % steps_per_seq > 0)
            prev_ref, prev_rows = kv_prev_ref, slice(0, blk)
        else:
            prev_ok = upper
            prev_ref, prev_rows = kv_cur_ref, slice((n - 1) * blk, n * blk)
        _attn_block(sink_ref, q_ref, o_ref, rows, prev_ref, prev_rows, kv_cur_ref,
                    upper, prev_ok, delta_f, low)


def _attn_block(sink_ref, q_ref, o_ref, rows, prev_ref, prev_rows, cur_ref, upper, prev_ok, delta_f, low):
    blk, dh = ATTN_BLOCK, ATTN_HEAD_DIM
    lanes = 2 * dh
    pairs = ATTN_GROUP // 2

    def block_diag(tile):
        cols = slice(tile * lanes, (tile + 1) * lanes)
        t = jnp.concatenate([prev_ref[prev_rows, cols], cur_ref[rows, cols]], axis=0)
        zero = jnp.zeros_like(t)
        return jnp.concatenate([jnp.where(low, t, zero), jnp.where(low, zero, t)], axis=0)

    def scores(kh):
        qs = jnp.concatenate([q_ref[rows, (kh * pairs + p) * lanes:(kh * pairs + p + 1) * lanes]
                              for p in range(pairs)], axis=0)
        k_bd = block_diag(kh)
        return lax.dot_general(qs, k_bd, (((1,), (1,)), ((), ())), preferred_element_type=F32)

    def softmax(kh, s_all):
        rows_out = []
        for p in range(pairs):
            parts = []
            for e in range(2):
                h = kh * ATTN_GROUP + 2 * p + e
                blk_s = s_all[p * blk:(p + 1) * blk, e * 2 * blk:(e + 1) * 2 * blk]
                s_prev = blk_s[:, :blk]
                if prev_ok is not upper:
                    s_prev = jnp.where(prev_ok, s_prev, -jnp.inf)
                s = jnp.where(upper, s_prev, blk_s[:, blk:])
                s = s * dh ** -0.5 - _ALIBI_SLOPES[h] * delta_f
                sink = sink_ref[0, h]
                mx = jnp.maximum(jnp.max(s, axis=-1, keepdims=True), sink)
                pr = jnp.exp(s - mx)
                denom = jnp.sum(pr, axis=-1, keepdims=True) + jnp.exp(sink - mx)
                probs = (pr / denom).astype(BF16)
                zero = jnp.zeros_like(probs)
                parts += [jnp.where(upper, probs, zero), jnp.where(upper, zero, probs)]
            rows_out.append(jnp.concatenate(parts, axis=1))
        return jnp.concatenate(rows_out, axis=0)

    def values(kh, probs):
        v_bd = block_diag(ATTN_KV_HEADS + kh)
        o = jnp.dot(probs, v_bd, preferred_element_type=F32)
        for p in range(pairs):
            cols = slice((kh * pairs + p) * lanes, (kh * pairs + p + 1) * lanes)
            o_ref[rows, cols] = o[p * blk:(p + 1) * blk, :].astype(o_ref.dtype)

    s_next = scores(0)
    for kh in range(ATTN_KV_HEADS):
        s_cur = s_next
        if kh + 1 < ATTN_KV_HEADS:
            s_next = scores(kh + 1)
        values(kh, softmax(kh, s_cur))


def _attention(q, kv, sinks, seq, blocks_per_step=4):
    m = q.shape[0]
    blk = ATTN_BLOCK
    rows = blocks_per_step * blk
    kvw = kv.shape[1]
    return pl.pallas_call(
        functools.partial(_attn_kernel, steps_per_seq=seq // rows),
        grid=(m // rows,),
        in_specs=[
            pl.BlockSpec(memory_space=pltpu.SMEM),
            pl.BlockSpec((rows, ATTN_Q_WIDTH), lambda i: (i, 0)),
            pl.BlockSpec((rows, kvw), lambda i: (i, 0)),
            pl.BlockSpec((blk, kvw), lambda i: (jnp.maximum(i * blocks_per_step - 1, 0), 0)),
        ],
        out_specs=pl.BlockSpec((rows, ATTN_Q_WIDTH), lambda i: (i, 0)),
        out_shape=jax.ShapeDtypeStruct((m, ATTN_Q_WIDTH), BF16),
        compiler_params=_params("parallel"),
        name="swa_attention",
    )(sinks, q, kv, kv)


def _dup_heads(a):
    lead = a.shape[:-1]
    a = a.reshape(*lead, ATTN_KV_HEADS, 1, ATTN_HEAD_DIM)
    a = jnp.broadcast_to(a, (*lead, ATTN_KV_HEADS, 2, ATTN_HEAD_DIM))
    return a.reshape(*lead, 2 * ATTN_KV_WIDTH)


def _ffn_kernel(x_ref, wg_ref, wv_ref, cp_ref, wo_ref, gain_ref, bias_ref, o_ref, xb_ref, tail_ref,
                *blocks, tiles_per_seq, sub_rows, sub_cols):
    n_blocks = (len(blocks) - FFN_STAGING_BUFFERS) // 2
    h_refs, act_refs = blocks[:n_blocks], blocks[n_blocks:2 * n_blocks]
    xs_refs = [blocks[2 * n_blocks + b % FFN_STAGING_BUFFERS] for b in range(n_blocks)]
    acc_ref = o_ref
    i = pl.program_id(0)
    j = pl.program_id(1)
    row_block, tf = act_refs[0].shape
    lane_tiles = xs_refs[0].shape[0]
    sub = F32_SUBLANES
    tiles = row_block // sub
    top = (CONV_WIDTH - 1) * sub

    def strided_rows(b, start, stride):
        return jnp.concatenate([xs_refs[b][c, pl.ds(start, sub, stride=stride), :]
                                for c in range(lane_tiles)], axis=1)

    def to_lane_tile_major(b, value):
        for c in range(lane_tiles):
            xs_refs[b][c, :, :] = value[:, c * LANES:(c + 1) * LANES]

    @pl.when(jnp.logical_and(i == 0, j == 0))
    def _():
        tail_ref[...] = jnp.zeros_like(tail_ref)

    def load_block(b):
        rows = slice(b * row_block, (b + 1) * row_block)
        to_lane_tile_major(b, x_ref[rows, :])
        x = jnp.concatenate([strided_rows(b, k, tiles) for k in range(tiles)], axis=0)
        xb_ref[rows, :] = x.astype(BF16)
        acc_ref[rows, :] = DEEPNORM_ALPHA * x

    def store_block(b):
        rows = slice(b * row_block, (b + 1) * row_block)
        to_lane_tile_major(b, _layer_norm(acc_ref[rows, :], gain_ref[...], bias_ref[...]))
        for m in range(tiles):
            start = sub * ((sub * m) % tiles) + (sub * m) // tiles
            o_ref[b * row_block + sub * m:b * row_block + sub * (m + 1), :] = strided_rows(b, start, sub)

    seq_start = i % tiles_per_seq == 0
    first_sublane = lax.broadcasted_iota(jnp.int32, (sub, tf), 0) == 0

    def in_proj(b, prev_last):
        xb = xb_ref[b * row_block:(b + 1) * row_block, :]
        last = []
        for half, w_ref in enumerate((wg_ref, wv_ref)):
            cols = slice(half * tf, (half + 1) * tf)
            h = jnp.dot(xb, w_ref[...], preferred_element_type=F32)
            h_refs[b][top:, cols] = h
            last.append(h[row_block - top:, :])
            for t in range(CONV_WIDTH - 1):
                rows = slice(t * sub, (t + 1) * sub)
                h_refs[b][rows, cols] = jnp.where(first_sublane,
                                                  pltpu.roll(prev_last[half][rows, :], 1, axis=0),
                                                  pltpu.roll(last[half][rows, :], 1, axis=0))
        return last

    def conv_act(b):
        h_ref = h_refs[b]
        for c0 in range(0, tf, sub_cols):
            cg = slice(c0, c0 + sub_cols)
            cv = slice(tf + c0, tf + c0 + sub_cols)
            for r0 in range(0, row_block, sub_rows):
                def conv(cols, p0):
                    return (cp_ref[p0 + 3:p0 + 4, cg]
                            + h_ref[top + r0 - 2 * sub:top + r0 - 2 * sub + sub_rows, cols] * cp_ref[p0:p0 + 1, cg]
                            + h_ref[top + r0 - sub:top + r0 - sub + sub_rows, cols] * cp_ref[p0 + 1:p0 + 2, cg]
                            + h_ref[top + r0:top + r0 + sub_rows, cols] * cp_ref[p0 + 2:p0 + 3, cg])
                gate = conv(cg, 0)
                val = conv(cv, CONV_WIDTH + 1)
                gelu = 0.5 * gate * (1.0 + lax.erf(gate * math.sqrt(0.5)))
                act_refs[b][r0:r0 + sub_rows, cg] = (gelu * val).astype(BF16)

    def out_proj(b):
        rows = slice(b * row_block, (b + 1) * row_block)
        acc_ref[rows, :] += jnp.dot(act_refs[b][...], wo_ref[...], preferred_element_type=F32)

    def chunk(first, last_chunk):
        tail = jnp.where(seq_start, 0.0, tail_ref[j])
        last = [tail[:, 0:tf], tail[:, tf:]]
        for step in range(n_blocks + 2):
            if step < n_blocks:
                if first:
                    load_block(step)
                last = in_proj(step, last)
            if 0 <= step - 1 < n_blocks:
                conv_act(step - 1)
            if 0 <= step - 2 < n_blocks:
                out_proj(step - 2)
                if last_chunk:
                    store_block(step - 2)
        tail_ref[j] = jnp.concatenate(last, axis=1)

    nf = pl.num_programs(1)
    pl.when(j == 0)(lambda: chunk(True, False))
    pl.when(jnp.logical_and(j > 0, j < nf - 1))(lambda: chunk(False, False))
    pl.when(j == nf - 1)(lambda: chunk(False, True))


FFN_CHUNK = 512


def _conv_params(conv_w, conv_b, tf):
    layers = conv_w.shape[0]
    p = jnp.concatenate([conv_w, conv_b[:, None, :]], axis=1)
    p = p.reshape(layers, CONV_WIDTH + 1, 2, D_FF // tf, tf)
    return p.transpose(0, 3, 2, 1, 4).reshape(layers, D_FF // tf, 2 * (CONV_WIDTH + 1), tf)


def _ffn(x, layer, w_in, conv_p, w_out, gain, bias, seq, tm=1024, tf=FFN_CHUNK,
         row_block=128, sub_rows=64, sub_cols=128):
    m, d = x.shape
    nf = D_FF // tf
    n_blocks = tm // row_block
    top = (CONV_WIDTH - 1) * F32_SUBLANES
    return pl.pallas_call(
        functools.partial(_ffn_kernel, tiles_per_seq=seq // tm, sub_rows=sub_rows, sub_cols=sub_cols),
        grid=(m // tm, nf),
        in_specs=[
            pl.BlockSpec((tm, d), lambda i, j: (i, 0)),
            pl.BlockSpec((None, d, tf), lambda i, j: (layer, 0, j)),
            pl.BlockSpec((None, d, tf), lambda i, j: (layer, 0, nf + j)),
            pl.BlockSpec((None, None, 2 * (CONV_WIDTH + 1), tf), lambda i, j: (layer, j, 0, 0)),
            pl.BlockSpec((None, tf, d), lambda i, j: (layer, j, 0)),
            pl.BlockSpec((1, d), lambda i, j: (0, 0)),
            pl.BlockSpec((1, d), lambda i, j: (0, 0)),
        ],
        out_specs=pl.BlockSpec((tm, d), lambda i, j: (i, 0)),
        out_shape=jax.ShapeDtypeStruct((m, d), F32),
        scratch_shapes=([pltpu.VMEM((tm, d), BF16),
                         pltpu.VMEM((nf, top, 2 * tf), F32)]
                        + [pltpu.VMEM((top + row_block, 2 * tf), F32)] * n_blocks
                        + [pltpu.VMEM((row_block, tf), BF16)] * n_blocks
                        + [pltpu.VMEM((d // LANES, row_block, LANES), F32)] * FFN_STAGING_BUFFERS),
        compiler_params=pltpu.CompilerParams(dimension_semantics=("arbitrary", "arbitrary"),
                                             vmem_limit_bytes=FFN_VMEM_LIMIT_BYTES),
        name="conv_glu_ffn",
    )(x, w_in, w_in, conv_p, w_out, gain, bias)


def kernel(x, ln_gain, ln_bias, even_w_in, pool_w, pool_scale, even_w_out, attn_w_qkv, attn_b_qkv, attn_sinks, attn_w_out, ffn_w_in, ffn_conv_w, ffn_conv_b, ffn_w_out):
    batch, seq, d = x.shape
    xf = x.reshape(batch * seq, d)
    row = lambda a: a.reshape(1, -1)
    even_w_in, even_w_out = even_w_in.astype(BF16), even_w_out.astype(BF16)
    attn_w_qkv, attn_w_out = attn_w_qkv.astype(BF16), attn_w_out.astype(BF16)
    ffn_w_in, ffn_w_out = ffn_w_in.astype(BF16), ffn_w_out.astype(BF16)
    ffn_conv_p = _conv_params(ffn_conv_w, ffn_conv_b, FFN_CHUNK)
    no_bias = jnp.zeros((1, even_w_in.shape[2]), F32)
    k_cols = slice(ATTN_Q_WIDTH, ATTN_Q_WIDTH + ATTN_KV_WIDTH)
    v_cols = slice(ATTN_Q_WIDTH + ATTN_KV_WIDTH, ATTN_Q_WIDTH + 2 * ATTN_KV_WIDTH)
    w_kv = jnp.concatenate([_dup_heads(attn_w_qkv[:, :, k_cols]), _dup_heads(attn_w_qkv[:, :, v_cols])], axis=2)
    for layer in range(DEPTH):
        li = layer // 2
        if layer % 2 == 0:
            qkv = _matmul(xf, even_w_in, li, no_bias, 0, 3 * RET_WIDTH, BF16)
            gu = _matmul(xf, even_w_in, li, no_bias, 3 * RET_WIDTH, RET_WIDTH + POOL_WIDTH, F32)
            y_ret = _retention(qkv, gu, batch, seq)
            y_pool = _pool(gu, pool_w[li].astype(BF16), row(pool_scale[li]), seq)
            xf = _proj_ln([y_ret, y_pool], even_w_out, li, xf,
                          row(ln_gain[layer, 0]), row(ln_bias[layer, 0]))
        else:
            b_qkv = row(attn_b_qkv[li])
            b_kv = jnp.concatenate([_dup_heads(b_qkv[:, k_cols]), _dup_heads(b_qkv[:, v_cols])], axis=1)
            q = _matmul(xf, attn_w_qkv, li, b_qkv, 0, ATTN_Q_WIDTH, BF16)
            kv = _matmul(xf, w_kv, li, b_kv, 0, 4 * ATTN_KV_WIDTH, BF16)
            o = _attention(q, kv, row(attn_sinks[li]), seq)
            xf = _proj_ln([o], attn_w_out, li, xf,
                          row(ln_gain[layer, 0]), row(ln_bias[layer, 0]))
        xf = _ffn(xf, layer, ffn_w_in, ffn_conv_p, ffn_w_out,
                  row(ln_gain[layer, 1]), row(ln_bias[layer, 1]), seq)
    return xf.reshape(batch, seq, d)
```

```python
import functools
import math

import numpy as np
import jax
import jax.numpy as jnp
from jax import lax
from jax.experimental import pallas as pl
from jax.experimental.pallas import tpu as pltpu

F32 = jnp.float32
BF16 = jnp.bfloat16

D_MODEL = 2048
DEPTH = 4
RET_HEAD_DIM = 256
RET_HEADS = 4
RET_WIDTH = RET_HEADS * RET_HEAD_DIM
RET_CHUNK = 128
POOL_WINDOWS = (2, 4, 8, 16)
POOL_GROUP = 256
POOL_WIDTH = len(POOL_WINDOWS) * POOL_GROUP
ATTN_HEAD_DIM = 64
ATTN_Q_HEADS = 32
ATTN_KV_HEADS = 4
ATTN_GROUP = ATTN_Q_HEADS // ATTN_KV_HEADS
ATTN_Q_WIDTH = ATTN_Q_HEADS * ATTN_HEAD_DIM
ATTN_KV_WIDTH = ATTN_KV_HEADS * ATTN_HEAD_DIM
WINDOW = 128
ATTN_BLOCK = 128
D_FF = 5632
CONV_WIDTH = 3
DEEPNORM_ALPHA = (2 * DEPTH) ** 0.25
LN_EPS = 1e-5

V7X_VMEM_BYTES = 64 * 1024 * 1024
VMEM_LIMIT_BYTES = V7X_VMEM_BYTES - 8 * 1024 * 1024
FFN_VMEM_LIMIT_BYTES = V7X_VMEM_BYTES - 3 * 1024 * 1024
FFN_STAGING_BUFFERS = 2
LANES = 128
F32_SUBLANES = 8
POOL_HALO = 16

_ALIBI_SLOPES = tuple(
    float(v) for v in (np.float32(2.0 ** (-8.0 / ATTN_Q_HEADS))
                       ** np.arange(1, ATTN_Q_HEADS + 1, dtype=np.float32)))


def _params(*semantics):
    return pltpu.CompilerParams(dimension_semantics=semantics,
                                vmem_limit_bytes=VMEM_LIMIT_BYTES)


def _layer_norm(z, gain, bias):
    mu = jnp.mean(z, axis=-1, keepdims=True)
    zc = z - mu
    var = jnp.mean(zc * zc, axis=-1, keepdims=True)
    return zc * lax.rsqrt(var + LN_EPS) * gain + bias


def _matmul_kernel(x_ref, w_ref, b_ref, o_ref, xb_ref):
    @pl.when(pl.program_id(1) == 0)
    def _():
        xb_ref[...] = x_ref[...].astype(BF16)

    acc = jnp.dot(xb_ref[...], w_ref[...], preferred_element_type=F32)
    o_ref[...] = (acc + b_ref[...]).astype(o_ref.dtype)


def _matmul(x, w, layer, b, col_start, n, out_dtype, tm=1024, tn=1024):
    m, k = x.shape
    off = col_start // tn
    assert col_start % tn == 0 and n % tn == 0
    return pl.pallas_call(
        _matmul_kernel,
        grid=(m // tm, n // tn),
        in_specs=[
            pl.BlockSpec((tm, k), lambda i, j: (i, 0)),
            pl.BlockSpec((None, k, tn), lambda i, j: (layer, 0, off + j)),
            pl.BlockSpec((1, tn), lambda i, j: (0, off + j)),
        ],
        out_specs=pl.BlockSpec((tm, tn), lambda i, j: (i, j)),
        out_shape=jax.ShapeDtypeStruct((m, n), out_dtype),
        scratch_shapes=[pltpu.VMEM((tm, k), BF16)],
        compiler_params=_params("parallel", "arbitrary"),
        name="proj_matmul",
    )(x, w, b)


def _retention_kernel(q_ref, k_ref, v_ref, g_ref, o_ref, state_ref, *, chunks_per_step):
    c_len, dh = RET_CHUNK, RET_HEAD_DIM

    @pl.when(pl.program_id(1) == 0)
    def _():
        state_ref[...] = jnp.zeros_like(state_ref)

    row = lax.broadcasted_iota(jnp.int32, (c_len, c_len), 0)
    col = lax.broadcasted_iota(jnp.int32, (c_len, c_len), 1)
    diff = (row - col).astype(F32)
    pos = lax.broadcasted_iota(jnp.int32, (c_len, dh), 0).astype(F32)
    k_scale = dh ** -0.5
    for h in range(RET_HEADS):
        log_gamma = math.log1p(-(2.0 ** (-5.0 - h)))
        decay = jnp.where(diff >= 0, jnp.exp(log_gamma * jnp.maximum(diff, 0.0)), 0.0) * k_scale
        zeta = jnp.exp(log_gamma * (c_len - 1.0 - pos)) * k_scale
        xi = jnp.exp(log_gamma * (pos + 1.0))
        chunk_decay = math.exp(log_gamma * c_len)
        cols = slice(h * dh, (h + 1) * dh)
        for c in range(chunks_per_step):
            rows = slice(c * c_len, (c + 1) * c_len)
            q = q_ref[rows, cols]
            k = k_ref[rows, cols]
            v = v_ref[rows, cols]
            state = state_ref[h]
            scores = lax.dot_general(q, k, (((1,), (1,)), ((), ())),
                                     preferred_element_type=F32) * decay
            inner = jnp.dot(scores.astype(BF16), v, preferred_element_type=F32)
            cross = jnp.dot(q, state.astype(BF16), preferred_element_type=F32) * xi
            kz = (k.astype(F32) * zeta).astype(BF16)
            kv = lax.dot_general(kz, v, (((0,), (0,)), ((), ())),
                                 preferred_element_type=F32)
            state_ref[h] = state * chunk_decay + kv
            y = inner + cross
            mu = jnp.mean(y, axis=-1, keepdims=True)
            yc = y - mu
            var = jnp.mean(yc * yc, axis=-1, keepdims=True)
            yn = yc * lax.rsqrt(var + LN_EPS)
            g = g_ref[rows, cols]
            o_ref[rows, cols] = (g * jax.nn.sigmoid(g) * yn).astype(o_ref.dtype)


def _retention(qkv, gu, batch, seq, chunks_per_step=8):
    m = qkv.shape[0]
    tr = chunks_per_step * RET_CHUNK
    steps = seq // tr
    row_map = lambda col: (lambda b, n: (b * steps + n, col))
    return pl.pallas_call(
        functools.partial(_retention_kernel, chunks_per_step=chunks_per_step),
        grid=(batch, steps),
        in_specs=[
            pl.BlockSpec((tr, RET_WIDTH), row_map(0)),
            pl.BlockSpec((tr, RET_WIDTH), row_map(1)),
            pl.BlockSpec((tr, RET_WIDTH), row_map(2)),
            pl.BlockSpec((tr, RET_WIDTH), row_map(0)),
        ],
        out_specs=pl.BlockSpec((tr, RET_WIDTH), row_map(0)),
        out_shape=jax.ShapeDtypeStruct((m, RET_WIDTH), BF16),
        scratch_shapes=[pltpu.VMEM((RET_HEADS, RET_HEAD_DIM, RET_HEAD_DIM), F32)],
        compiler_params=_params("parallel", "arbitrary"),
        name="retention",
    )(qkv, qkv, qkv, gu)


def _pool_kernel(u_ref, halo_ref, w_ref, scale_ref, o_ref, *, tiles_per_seq):
    tr = u_ref.shape[0]
    tile_in_seq = pl.program_id(0) % tiles_per_seq
    halo_on = (tile_in_seq > 0).astype(F32)
    t = tile_in_seq * tr + lax.broadcasted_iota(jnp.int32, (tr, POOL_GROUP), 0)
    for gi, win in enumerate(POOL_WINDOWS):
        cols = slice(gi * POOL_GROUP, (gi + 1) * POOL_GROUP)
        u = u_ref[:, cols]
        ext = jnp.concatenate([halo_ref[:, cols] * halo_on, u], axis=0)
        span = 1
        while span < win:
            ext = ext + pltpu.roll(ext, span, axis=0)
            span *= 2
        count = jnp.minimum(t + 1, win).astype(F32)
        pooled = ext[POOL_HALO:, :] / count - u
        y = jnp.dot(pooled.astype(BF16), w_ref[gi], preferred_element_type=F32)
        o_ref[:, cols] = (y * scale_ref[:, cols]).astype(o_ref.dtype)


def _pool(gu, pool_w, pool_scale, seq, tr=512):
    m = gu.shape[0]
    halo_blocks = tr // POOL_HALO
    return pl.pallas_call(
        functools.partial(_pool_kernel, tiles_per_seq=seq // tr),
        grid=(m // tr,),
        in_specs=[
            pl.BlockSpec((tr, POOL_WIDTH), lambda i: (i, 1)),
            pl.BlockSpec((POOL_HALO, POOL_WIDTH),
                         lambda i: (jnp.maximum(i * halo_blocks - 1, 0), 1)),
            pl.BlockSpec((len(POOL_WINDOWS), POOL_GROUP, POOL_GROUP), lambda i: (0, 0, 0)),
            pl.BlockSpec((1, POOL_WIDTH), lambda i: (0, 0)),
        ],
        out_specs=pl.BlockSpec((tr, POOL_WIDTH), lambda i: (i, 0)),
        out_shape=jax.ShapeDtypeStruct((m, POOL_WIDTH), BF16),
        compiler_params=_params("parallel"),
        name="pool",
    )(gu, gu, pool_w, pool_scale)


def _proj_ln_kernel(*refs, n_in, row_block):
    y_refs, w_refs = refs[:n_in], refs[n_in:2 * n_in]
    x_ref, gain_ref, bias_ref, o_ref = refs[2 * n_in:]
    for r0 in range(0, x_ref.shape[0], row_block):
        rows = slice(r0, r0 + row_block)
        acc = jnp.dot(y_refs[0][rows, :], w_refs[0][...], preferred_element_type=F32)
        for y_ref, w_ref in zip(y_refs[1:], w_refs[1:]):
            acc = acc + jnp.dot(y_ref[rows, :], w_ref[...], preferred_element_type=F32)
        z = DEEPNORM_ALPHA * x_ref[rows, :] + acc
        o_ref[rows, :] = _layer_norm(z, gain_ref[...], bias_ref[...])


def _proj_ln(ys, w, layer, x, gain, bias, tm=1024, row_block=128):
    m, d = x.shape
    n_in = len(ys)
    width = ys[0].shape[1]
    assert all(y.shape[1] == width for y in ys) and n_in * width == w.shape[1]
    in_specs = [pl.BlockSpec((tm, width), lambda i: (i, 0)) for _ in ys]
    in_specs += [pl.BlockSpec((None, width, d), lambda i, s=s: (layer, s, 0), pipeline_mode=pl.Buffered(1))
                 for s in range(n_in)]
    in_specs += [
        pl.BlockSpec((tm, d), lambda i: (i, 0)),
        pl.BlockSpec((1, d), lambda i: (0, 0)),
        pl.BlockSpec((1, d), lambda i: (0, 0)),
    ]
    return pl.pallas_call(
        functools.partial(_proj_ln_kernel, n_in=n_in, row_block=row_block),
        grid=(m // tm,),
        in_specs=in_specs,
        out_specs=pl.BlockSpec((tm, d), lambda i: (i, 0)),
        out_shape=jax.ShapeDtypeStruct((m, d), F32),
        compiler_params=_params("parallel"),
        name="proj_ln",
    )(*ys, *([w] * n_in), x, gain, bias)


def _attn_kernel(sink_ref, q_ref, kv_cur_ref, kv_prev_ref, o_ref, *, steps_per_seq):
    blk, dh = ATTN_BLOCK, ATTN_HEAD_DIM
    lanes = 2 * dh
    pairs = ATTN_GROUP // 2
    i = lax.broadcasted_iota(jnp.int32, (blk, blk), 0)
    j = lax.broadcasted_iota(jnp.int32, (blk, blk), 1)
    upper = j > i
    delta_f = jnp.where(upper, blk + i - j, i - j).astype(F32)
    lane = lax.broadcasted_iota(jnp.int32, (2 * blk, lanes), 1)
    low = lane < dh
    for n in range(q_ref.shape[0] // blk):
        rows = slice(n * blk, (n + 1) * blk)
        if n == 0:
            prev_ok = upper & (pl.program_id(0) % steps_per_seq > 0)
            prev_ref, prev_rows = kv_prev_ref, slice(0, blk)
        else:
            prev_ok = upper
            prev_ref, prev_rows = kv_cur_ref, slice((n - 1) * blk, n * blk)
        _attn_block(sink_ref, q_ref, o_ref, rows, prev_ref, prev_rows, kv_cur_ref,
                    upper, prev_ok, delta_f, low)


def _attn_block(sink_ref, q_ref, o_ref, rows, prev_ref, prev_rows, cur_ref, upper, prev_ok, delta_f, low):
    blk, dh = ATTN_BLOCK, ATTN_HEAD_DIM
    lanes = 2 * dh
    pairs = ATTN_GROUP // 2

    def block_diag(tile):
        cols = slice(tile * lanes, (tile + 1) * lanes)
        t = jnp.concatenate([prev_ref[prev_rows, cols], cur_ref[rows, cols]], axis=0)
        zero = jnp.zeros_like(t)
        return jnp.concatenate([jnp.where(low, t, zero), jnp.where(low, zero, t)], axis=0)

    def scores(kh):
        qs = jnp.concatenate([q_ref[rows, (kh * pairs + p) * lanes:(kh * pairs + p + 1) * lanes]
                              for p in range(pairs)], axis=0)
        k_bd = block_diag(kh)
        return lax.dot_general(qs, k_bd, (((1,), (1,)), ((), ())), preferred_element_type=F32)

    def softmax(kh, s_all):
        rows_out = []
        for p in range(pairs):
            parts = []
            for e in range(2):
                h = kh * ATTN_GROUP + 2 * p + e
                blk_s = s_all[p * blk:(p + 1) * blk, e * 2 * blk:(e + 1) * 2 * blk]
                s_prev = blk_s[:, :blk]
                if prev_ok is not upper:
                    s_prev = jnp.where(prev_ok, s_prev, -jnp.inf)
                s = jnp.where(upper, s_prev, blk_s[:, blk:])
                s = s * dh ** -0.5 - _ALIBI_SLOPES[h] * delta_f
                sink = sink_ref[0, h]
                mx = jnp.maximum(jnp.max(s, axis=-1, keepdims=True), sink)
                pr = jnp.exp(s - mx)
                denom = jnp.sum(pr, axis=-1, keepdims=True) + jnp.exp(sink - mx)
                probs = (pr / denom).astype(BF16)
                zero = jnp.zeros_like(probs)
                parts += [jnp.where(upper, probs, zero), jnp.where(upper, zero, probs)]
            rows_out.append(jnp.concatenate(parts, axis=1))
        return jnp.concatenate(rows_out, axis=0)

    def values(kh, probs):
        v_bd = block_diag(ATTN_KV_HEADS + kh)
        o = jnp.dot(probs, v_bd, preferred_element_type=F32)
        for p in range(pairs):
            cols = slice((kh * pairs + p) * lanes, (kh * pairs + p + 1) * lanes)
            o_ref[rows, cols] = o[p * blk:(p + 1) * blk, :].astype(o_ref.dtype)

    s_next = scores(0)
    for kh in range(ATTN_KV_HEADS):
        s_cur = s_next
        if kh + 1 < ATTN_KV_HEADS:
            s_next = scores(kh + 1)
        values(kh, softmax(kh, s_cur))


def _attention(q, kv, sinks, seq, blocks_per_step=8):
    m = q.shape[0]
    blk = ATTN_BLOCK
    rows = blocks_per_step * blk
    kvw = kv.shape[1]
    return pl.pallas_call(
        functools.partial(_attn_kernel, steps_per_seq=seq // rows),
        grid=(m // rows,),
        in_specs=[
            pl.BlockSpec(memory_space=pltpu.SMEM),
            pl.BlockSpec((rows, ATTN_Q_WIDTH), lambda i: (i, 0)),
            pl.BlockSpec((rows, kvw), lambda i: (i, 0)),
            pl.BlockSpec((blk, kvw), lambda i: (jnp.maximum(i * blocks_per_step - 1, 0), 0)),
        ],
        out_specs=pl.BlockSpec((rows, ATTN_Q_WIDTH), lambda i: (i, 0)),
        out_shape=jax.ShapeDtypeStruct((m, ATTN_Q_WIDTH), BF16),
        compiler_params=_params("parallel"),
        name="swa_attention",
    )(sinks, q, kv, kv)


def _dup_heads(a):
    lead = a.shape[:-1]
    a = a.reshape(*lead, ATTN_KV_HEADS, 1, ATTN_HEAD_DIM)
    a = jnp.broadcast_to(a, (*lead, ATTN_KV_HEADS, 2, ATTN_HEAD_DIM))
    return a.reshape(*lead, 2 * ATTN_KV_WIDTH)


def _ffn_kernel(x_ref, wg_ref, wv_ref, cp_ref, wo_ref, gain_ref, bias_ref, o_ref, xb_ref, tail_ref,
                *blocks, tiles_per_seq, sub_rows, sub_cols):
    n_blocks = (len(blocks) - FFN_STAGING_BUFFERS) // 2
    h_refs, act_refs = blocks[:n_blocks], blocks[n_blocks:2 * n_blocks]
    xs_refs = [blocks[2 * n_blocks + b % FFN_STAGING_BUFFERS] for b in range(n_blocks)]
    acc_ref = o_ref
    i = pl.program_id(0)
    j = pl.program_id(1)
    row_block, tf = act_refs[0].shape
    lane_tiles = xs_refs[0].shape[0]
    sub = F32_SUBLANES
    tiles = row_block // sub
    top = (CONV_WIDTH - 1) * sub

    def strided_rows(b, start, stride):
        return jnp.concatenate([xs_refs[b][c, pl.ds(start, sub, stride=stride), :]
                                for c in range(lane_tiles)], axis=1)

    def to_lane_tile_major(b, value):
        for c in range(lane_tiles):
            xs_refs[b][c, :, :] = value[:, c * LANES:(c + 1) * LANES]

    @pl.when(jnp.logical_and(i == 0, j == 0))
    def _():
        tail_ref[...] = jnp.zeros_like(tail_ref)

    def load_block(b):
        rows = slice(b * row_block, (b + 1) * row_block)
        to_lane_tile_major(b, x_ref[rows, :])
        x = jnp.concatenate([strided_rows(b, k, tiles) for k in range(tiles)], axis=0)
        xb_ref[rows, :] = x.astype(BF16)
        acc_ref[rows, :] = DEEPNORM_ALPHA * x

    def store_block(b):
        rows = slice(b * row_block, (b + 1) * row_block)
        to_lane_tile_major(b, _layer_norm(acc_ref[rows, :], gain_ref[...], bias_ref[...]))
        for m in range(tiles):
            start = sub * ((sub * m) % tiles) + (sub * m) // tiles
            o_ref[b * row_block + sub * m:b * row_block + sub * (m + 1), :] = strided_rows(b, start, sub)

    seq_start = i % tiles_per_seq == 0
    first_sublane = lax.broadcasted_iota(jnp.int32, (sub, tf), 0) == 0

    def in_proj(b, prev_last):
        xb = xb_ref[b * row_block:(b + 1) * row_block, :]
        last = []
        for half, w_ref in enumerate((wg_ref, wv_ref)):
            cols = slice(half * tf, (half + 1) * tf)
            h = jnp.dot(xb, w_ref[...], preferred_element_type=F32)
            h_refs[b][top:, cols] = h
            last.append(h[row_block - top:, :])
            for t in range(CONV_WIDTH - 1):
                rows = slice(t * sub, (t + 1) * sub)
                h_refs[b][rows, cols] = jnp.where(first_sublane,
                                                  pltpu.roll(prev_last[half][rows, :], 1, axis=0),
                                                  pltpu.roll(last[half][rows, :], 1, axis=0))
        return last

    def conv_act(b):
        h_ref = h_refs[b]
        for c0 in range(0, tf, sub_cols):
            cg = slice(c0, c0 + sub_cols)
            cv = slice(tf + c0, tf + c0 + sub_cols)
            for r0 in range(0, row_block, sub_rows):
                def conv(cols, p0):
                    return (cp_ref[p0 + 3:p0 + 4, cg]
                            + h_ref[top + r0 - 2 * sub:top + r0 - 2 * sub + sub_rows, cols] * cp_ref[p0:p0 + 1, cg]
                            + h_ref[top + r0 - sub:top + r0 - sub + sub_rows, cols] * cp_ref[p0 + 1:p0 + 2, cg]
                            + h_ref[top + r0:top + r0 + sub_rows, cols] * cp_ref[p0 + 2:p0 + 3, cg])
                gate = conv(cg, 0)
                val = conv(cv, CONV_WIDTH + 1)
                gelu = 0.5 * gate * (1.0 + lax.erf(gate * math.sqrt(0.5)))
                act_refs[b][r0:r0 + sub_rows, cg] = (gelu * val).astype(BF16)

    def out_proj(b):
        rows = slice(b * row_block, (b + 1) * row_block)
        acc_ref[rows, :] += jnp.dot(act_refs[b][...], wo_ref[...], preferred_element_type=F32)

    def chunk(first, last_chunk):
        tail = jnp.where(seq_start, 0.0, tail_ref[j])
        last = [tail[:, 0:tf], tail[:, tf:]]
        for step in range(n_blocks + 2):
            if step < n_blocks:
                if first:
                    load_block(step)
                last = in_proj(step, last)
            if 0 <= step - 1 < n_blocks:
                conv_act(step - 1)
            if 0 <= step - 2 < n_blocks:
                out_proj(step - 2)
                if last_chunk:
                    store_block(step - 2)
        tail_ref[j] = jnp.concatenate(last, axis=1)

    nf = pl.num_programs(1)
    pl.when(j == 0)(lambda: chunk(True, False))
    pl.when(jnp.logical_and(j > 0, j < nf - 1))(lambda: chunk(False, False))
    pl.when(j == nf - 1)(lambda: chunk(False, True))


FFN_CHUNK = 512


def _conv_params(conv_w, conv_b, tf):
    layers = conv_w.shape[0]
    p = jnp.concatenate([conv_w, conv_b[:, None, :]], axis=1)
    p = p.reshape(layers, CONV_WIDTH + 1, 2, D_FF // tf, tf)
    return p.transpose(0, 3, 2, 1, 4).reshape(layers, D_FF // tf, 2 * (CONV_WIDTH + 1), tf)


def _ffn(x, layer, w_in, conv_p, w_out, gain, bias, seq, tm=1024, tf=FFN_CHUNK,
         row_block=128, sub_rows=32, sub_cols=256):
    m, d = x.shape
    nf = D_FF // tf
    n_blocks = tm // row_block
    top = (CONV_WIDTH - 1) * F32_SUBLANES
    return pl.pallas_call(
        functools.partial(_ffn_kernel, tiles_per_seq=seq // tm, sub_rows=sub_rows, sub_cols=sub_cols),
        grid=(m // tm, nf),
        in_specs=[
            pl.BlockSpec((tm, d), lambda i, j: (i, 0)),
            pl.BlockSpec((None, d, tf), lambda i, j: (layer, 0, j)),
            pl.BlockSpec((None, d, tf), lambda i, j: (layer, 0, nf + j)),
            pl.BlockSpec((None, None, 2 * (CONV_WIDTH + 1), tf), lambda i, j: (layer, j, 0, 0)),
            pl.BlockSpec((None, tf, d), lambda i, j: (layer, j, 0)),
            pl.BlockSpec((1, d), lambda i, j: (0, 0)),
            pl.BlockSpec((1, d), lambda i, j: (0, 0)),
        ],
        out_specs=pl.BlockSpec((tm, d), lambda i, j: (i, 0)),
        out_shape=jax.ShapeDtypeStruct((m, d), F32),
        scratch_shapes=([pltpu.VMEM((tm, d), BF16),
                         pltpu.VMEM((nf, top, 2 * tf), F32)]
                        + [pltpu.VMEM((top + row_block, 2 * tf), F32)] * n_blocks
                        + [pltpu.VMEM((row_block, tf), BF16)] * n_blocks
                        + [pltpu.VMEM((d // LANES, row_block, LANES), F32)] * FFN_STAGING_BUFFERS),
        compiler_params=pltpu.CompilerParams(dimension_semantics=("arbitrary", "arbitrary"),
                                             vmem_limit_bytes=FFN_VMEM_LIMIT_BYTES),
        name="conv_glu_ffn",
    )(x, w_in, w_in, conv_p, w_out, gain, bias)


def kernel(x, ln_gain, ln_bias, even_w_in, pool_w, pool_scale, even_w_out, attn_w_qkv, attn_b_qkv, attn_sinks, attn_w_out, ffn_w_in, ffn_conv_w, ffn_conv_b, ffn_w_out):
    batch, seq, d = x.shape
    xf = x.reshape(batch * seq, d)
    row = lambda a: a.reshape(1, -1)
    even_w_in, even_w_out = even_w_in.astype(BF16), even_w_out.astype(BF16)
    attn_w_qkv, attn_w_out = attn_w_qkv.astype(BF16), attn_w_out.astype(BF16)
    ffn_w_in, ffn_w_out = ffn_w_in.astype(BF16), ffn_w_out.astype(BF16)
    ffn_conv_p = _conv_params(ffn_conv_w, ffn_conv_b, FFN_CHUNK)
    no_bias = jnp.zeros((1, even_w_in.shape[2]), F32)
    k_cols = slice(ATTN_Q_WIDTH, ATTN_Q_WIDTH + ATTN_KV_WIDTH)
    v_cols = slice(ATTN_Q_WIDTH + ATTN_KV_WIDTH, ATTN_Q_WIDTH + 2 * ATTN_KV_WIDTH)
    w_kv = jnp.concatenate([_dup_heads(attn_w_qkv[:, :, k_cols]), _dup_heads(attn_w_qkv[:, :, v_cols])], axis=2)
    for layer in range(DEPTH):
        li = layer // 2
        if layer % 2 == 0:
            qkv = _matmul(xf, even_w_in, li, no_bias, 0, 3 * RET_WIDTH, BF16, tn=3 * RET_WIDTH // 2)
            gu = _matmul(xf, even_w_in, li, no_bias, 3 * RET_WIDTH, RET_WIDTH + POOL_WIDTH, F32)
            y_ret = _retention(qkv, gu, batch, seq)
            y_pool = _pool(gu, pool_w[li].astype(BF16), row(pool_scale[li]), seq)
            xf = _proj_ln([y_ret, y_pool], even_w_out, li, xf,
                          row(ln_gain[layer, 0]), row(ln_bias[layer, 0]))
        else:
            b_qkv = row(attn_b_qkv[li])
            b_kv = jnp.concatenate([_dup_heads(b_qkv[:, k_cols]), _dup_heads(b_qkv[:, v_cols])], axis=1)
            q = _matmul(xf, attn_w_qkv, li, b_qkv, 0, ATTN_Q_WIDTH, BF16, tn=ATTN_Q_WIDTH)
            kv = _matmul(xf, w_kv, li, b_kv, 0, 4 * ATTN_KV_WIDTH, BF16)
            o = _attention(q, kv, row(attn_sinks[li]), seq)
            xf = _proj_ln([o], attn_w_out, li, xf,
                          row(ln_gain[layer, 0]), row(ln_bias[layer, 0]))
        xf = _ffn(xf, layer, ffn_w_in, ffn_conv_p, ffn_w_out,
                  row(ln_gain[layer, 1]), row(ln_bias[layer, 1]), seq)
    return xf.reshape(batch, seq, d)
```
